```python
import jax
import jax.numpy as jnp
from jax import lax
import numpy as np

D_MODEL = 2048
BATCH = 2
SEQ = 16384
DEPTH = 4

N_A_LAYERS = DEPTH // 2
N_B_LAYERS = DEPTH - N_A_LAYERS
GDN_QK_HEADS = 16
GDN_V_HEADS = 32
GDN_HEAD_DIM = 128
GDN_CONV = 4
GDN_CHUNK = 64
GDN_QK_DIM = GDN_QK_HEADS * GDN_HEAD_DIM
GDN_V_DIM = GDN_V_HEADS * GDN_HEAD_DIM
GDN_CONV_CH = 2 * GDN_QK_DIM + GDN_V_DIM
GDN_PROJ = GDN_CONV_CH + GDN_V_DIM + 2 * GDN_V_HEADS
SB_HEADS = 16
SB_HEAD_DIM = D_MODEL // SB_HEADS
SB_BLOCK = 128
FFN_DIM = 5632
FFN_HALF = 0.5
N_SUB = 3
EPS = 1e-6

kernel_name = 'yoco_gdn_stickbreak_macaron_adaln'


def rms_norm(x, g):
    xf = x.astype(jnp.float32)
    y = xf * lax.rsqrt(jnp.mean(xf * xf, axis=-1, keepdims=True) + EPS)
    return (y * g.astype(jnp.float32)).astype(x.dtype)


def l2_normalize(x):
    return x * lax.rsqrt(jnp.sum(x * x, axis=-1, keepdims=True) + EPS)


def modulate(h, shift, scale):
    return h * (1.0 + scale[:, None, :]) + shift[:, None, :]


def swiglu(h, w_gu, w_down):
    gate, up = jnp.split(h @ w_gu, 2, axis=-1)
    return (jax.nn.silu(gate) * up) @ w_down


def causal_depthwise_conv(x, w):
    ch = x.shape[-1]
    return lax.conv_general_dilated(x, w[:, None, :].astype(x.dtype), window_strides=(1,),
                                    padding=[(w.shape[0] - 1, 0)],
                                    dimension_numbers=('NWC', 'WIO', 'NWC'),
                                    feature_group_count=ch)


def gated_delta_rule(q, k, v, g, beta):
    b, s, h, dk = q.shape
    dv = v.shape[-1]
    C = GDN_CHUNK
    n = s // C

    def chunks(t):
        t = jnp.moveaxis(t, 2, 1)
        return t.reshape(t.shape[:2] + (n, C) + t.shape[3:])

    q, k, v, g, beta = (chunks(t) for t in (q, k, v, g, beta))
    q = q * dk ** -0.5
    G = jnp.cumsum(g, axis=-1)
    idx = jnp.arange(C)
    incl = idx[:, None] >= idx[None, :]
    strict = idx[:, None] > idx[None, :]
    decay = jnp.exp(jnp.where(incl, G[..., :, None] - G[..., None, :], -jnp.inf))
    kb = k * beta[..., None]
    m = jnp.einsum('bhnid,bhnjd->bhnij', kb, k) * jnp.where(strict, decay, 0.0)
    eye_plus_m = m + jnp.eye(C, dtype=m.dtype)
    rhs = jnp.concatenate([v * beta[..., None], kb * jnp.exp(G)[..., None]], axis=-1)
    uw = lax.linalg.triangular_solve(eye_plus_m, rhs, left_side=True, lower=True, unit_diagonal=True)
    u, w = uw[..., :dv], uw[..., dv:]
    attn = jnp.einsum('bhnid,bhnjd->bhnij', q, k) * decay
    qg = q * jnp.exp(G)[..., None]
    kd = k * jnp.exp(G[..., -1:] - G)[..., None]
    gl = jnp.exp(G[..., -1])
    xs = tuple(jnp.moveaxis(t, 2, 0) for t in (qg, kd, u, w, attn, gl))

    def step(state, xs_n):
        qg_n, kd_n, u_n, w_n, attn_n, gl_n = xs_n
        v_new = u_n - jnp.einsum('bhcd,bhde->bhce', w_n, state)
        o = jnp.einsum('bhcd,bhde->bhce', qg_n, state) + jnp.einsum('bhij,bhje->bhie', attn_n, v_new)
        state = state * gl_n[..., None, None] + jnp.einsum('bhcd,bhce->bhde', kd_n, v_new)
        return state, o

    s0 = jnp.zeros((b, h, dk, dv), jnp.float32)
    _, o = lax.scan(step, s0, xs)
    o = jnp.moveaxis(o, 0, 2).reshape(b, h, s, dv)
    return jnp.moveaxis(o, 1, 2)


def gdn_mixer(h, w_in, conv_w, a_log, dt_bias, onorm_g, w_out):
    b, s, _ = h.shape
    proj = h @ w_in
    qkv, z, beta_raw, a_raw = jnp.split(
        proj, [GDN_CONV_CH, GDN_CONV_CH + GDN_V_DIM, GDN_CONV_CH + GDN_V_DIM + GDN_V_HEADS], axis=-1)
    qkv = jax.nn.silu(causal_depthwise_conv(qkv, conv_w)).astype(jnp.float32)
    q, k, v = jnp.split(qkv, [GDN_QK_DIM, 2 * GDN_QK_DIM], axis=-1)
    rep = GDN_V_HEADS // GDN_QK_HEADS
    q = jnp.repeat(l2_normalize(q.reshape(b, s, GDN_QK_HEADS, GDN_HEAD_DIM)), rep, axis=2)
    k = jnp.repeat(l2_normalize(k.reshape(b, s, GDN_QK_HEADS, GDN_HEAD_DIM)), rep, axis=2)
    v = v.reshape(b, s, GDN_V_HEADS, GDN_HEAD_DIM)
    beta = jax.nn.sigmoid(beta_raw.astype(jnp.float32))
    g = -jnp.exp(a_log.astype(jnp.float32)) * jax.nn.softplus(
        a_raw.astype(jnp.float32) + dt_bias.astype(jnp.float32))
    o = gated_delta_rule(q, k, v, g, beta)
    zf = z.reshape(b, s, GDN_V_HEADS, GDN_HEAD_DIM).astype(jnp.float32)
    o = rms_norm(o, onorm_g) * jax.nn.silu(zf)
    return o.reshape(b, s, GDN_V_DIM).astype(h.dtype) @ w_out


def shared_kv(x, c_act, kv_ada_w, kv_ada_b, kv_norm_g, kv_w, k_norm_g):
    b, s, _ = x.shape
    shift, scale = jnp.split(c_act @ kv_ada_w + kv_ada_b, 2, axis=-1)
    h = modulate(rms_norm(x, kv_norm_g), shift, scale)
    k, v = jnp.split(h @ kv_w, 2, axis=-1)
    k = rms_norm(k.reshape(b, s, SB_HEADS, SB_HEAD_DIM), k_norm_g)
    v = v.reshape(b, s, SB_HEADS, SB_HEAD_DIM)
    return jnp.transpose(k, (0, 2, 1, 3)), jnp.transpose(v, (0, 2, 1, 3))


def stick_breaking_attention(q, k, v):
    b, h, s, dh = q.shape
    L = SB_BLOCK
    nb = s // L
    scale = dh ** -0.5
    qblocks = jnp.moveaxis(q.reshape(b, h, nb, L, dh), 2, 0)
    offs = jnp.arange(L)

    def one_block(args):
        qb, i = args
        qpos = i * L + offs

        def body(j, carry):
            acc, logrem = carry
            kb_idx = i - j
            ks = lax.dynamic_slice_in_dim(k, kb_idx * L, L, axis=2)
            vs = lax.dynamic_slice_in_dim(v, kb_idx * L, L, axis=2)
            z = jnp.einsum('bhqd,bhkd->bhqk', qb, ks, preferred_element_type=jnp.float32) * scale
            kpos = kb_idx * L + offs
            mask = kpos[None, :] < qpos[:, None]
            log_beta = jax.nn.log_sigmoid(z)
            log_1mb = jnp.where(mask, jax.nn.log_sigmoid(-z), 0.0)
            right = lax.cumsum(log_1mb, axis=3, reverse=True) - log_1mb
            wts = jnp.where(mask, jnp.exp(log_beta + right + logrem[..., None]), 0.0)
            acc = acc + jnp.einsum('bhqk,bhkd->bhqd', wts, vs.astype(jnp.float32))
            logrem = logrem + jnp.sum(log_1mb, axis=3)
            return acc, logrem

        init = (jnp.zeros((b, h, L, dh), jnp.float32), jnp.zeros((b, h, L), jnp.float32))
        acc, _ = lax.fori_loop(0, i + 1, body, init)
        return acc

    out = lax.map(one_block, (qblocks, jnp.arange(nb)))
    return jnp.moveaxis(out, 0, 2).reshape(b, h, s, dh)


def sb_mixer(h, k_sh, v_sh, w_q, q_norm_g, w_out):
    b, s, _ = h.shape
    q = rms_norm((h @ w_q).reshape(b, s, SB_HEADS, SB_HEAD_DIM), q_norm_g)
    o = stick_breaking_attention(jnp.transpose(q, (0, 2, 1, 3)), k_sh, v_sh)
    o = jnp.transpose(o, (0, 2, 1, 3)).reshape(b, s, D_MODEL).astype(h.dtype)
    return o @ w_out


def setup_inputs(seed: int = 0) -> dict:
    key = jax.random.key(seed)
    ks = jax.random.split(key, 24)
    nrm = jax.random.normal
    d = D_MODEL
    ada_scale = 0.3 * d ** -0.5
    dt = jnp.exp(jax.random.uniform(ks[9], (N_A_LAYERS, GDN_V_HEADS)) * (jnp.log(0.1) - jnp.log(0.001)) + jnp.log(0.001))
    return {
        'x': nrm(ks[0], (BATCH, SEQ, d), jnp.float32),
        'c': nrm(ks[1], (BATCH, d), jnp.float32),
        'ada_w': nrm(ks[2], (DEPTH, d, N_SUB * 3 * d), jnp.float32) * ada_scale,
        'ada_b': nrm(ks[3], (DEPTH, N_SUB * 3 * d), jnp.float32) * 0.02,
        'norm_g': 1.0 + 0.02 * nrm(ks[4], (DEPTH, N_SUB, d), jnp.float32),
        'ffn_w_gu': nrm(ks[5], (DEPTH, 2, d, 2 * FFN_DIM), jnp.float32) * d ** -0.5,
        'ffn_w_down': nrm(ks[6], (DEPTH, 2, FFN_DIM, d), jnp.float32) * FFN_DIM ** -0.5,
        'gdn_w_in': nrm(ks[7], (N_A_LAYERS, d, GDN_PROJ), jnp.float32) * d ** -0.5,
        'gdn_conv_w': nrm(ks[8], (N_A_LAYERS, GDN_CONV, GDN_CONV_CH), jnp.float32) * GDN_CONV ** -0.5,
        'gdn_a_log': jnp.log(jax.random.uniform(ks[10], (N_A_LAYERS, GDN_V_HEADS), jnp.float32, 1.0, 16.0)),
        'gdn_dt_bias': dt + jnp.log(-jnp.expm1(-dt)),
        'gdn_onorm_g': 1.0 + 0.02 * nrm(ks[11], (N_A_LAYERS, GDN_HEAD_DIM), jnp.float32),
        'gdn_w_out': nrm(ks[12], (N_A_LAYERS, GDN_V_DIM, d), jnp.float32) * GDN_V_DIM ** -0.5,
        'kv_ada_w': nrm(ks[13], (d, 2 * d), jnp.float32) * ada_scale,
        'kv_ada_b': nrm(ks[14], (2 * d,), jnp.float32) * 0.02,
        'kv_norm_g': 1.0 + 0.02 * nrm(ks[15], (d,), jnp.float32),
        'kv_w': nrm(ks[16], (d, 2 * d), jnp.float32) * d ** -0.5,
        'k_norm_g': 1.0 + 0.02 * nrm(ks[17], (SB_HEAD_DIM,), jnp.float32),
        'sb_w_q': nrm(ks[18], (N_B_LAYERS, d, d), jnp.float32) * d ** -0.5,
        'sb_q_norm_g': 1.0 + 0.02 * nrm(ks[19], (N_B_LAYERS, SB_HEAD_DIM), jnp.float32),
        'sb_w_out': nrm(ks[20], (N_B_LAYERS, d, d), jnp.float32) * d ** -0.5,
    }


def reference(x, c, ada_w, ada_b, norm_g, ffn_w_gu, ffn_w_down, gdn_w_in, gdn_conv_w, gdn_a_log,
              gdn_dt_bias, gdn_onorm_g, gdn_w_out, kv_ada_w, kv_ada_b, kv_norm_g, kv_w, k_norm_g,
              sb_w_q, sb_q_norm_g, sb_w_out):
    b = x.shape[0]
    c_act = jax.nn.silu(c)
    k_sh = None
    v_sh = None
    for l in range(DEPTH):
        mod = (c_act @ ada_w[l] + ada_b[l]).reshape(b, N_SUB, 3, D_MODEL)
        h = modulate(rms_norm(x, norm_g[l, 0]), mod[:, 0, 0], mod[:, 0, 1])
        x = x + FFN_HALF * (1.0 + mod[:, 0, 2])[:, None, :] * swiglu(h, ffn_w_gu[l, 0], ffn_w_down[l, 0])
        h = modulate(rms_norm(x, norm_g[l, 1]), mod[:, 1, 0], mod[:, 1, 1])
        if l < N_A_LAYERS:
            y = gdn_mixer(h, gdn_w_in[l], gdn_conv_w[l], gdn_a_log[l], gdn_dt_bias[l], gdn_onorm_g[l], gdn_w_out[l])
        else:
            j = l - N_A_LAYERS
            y = sb_mixer(h, k_sh, v_sh, sb_w_q[j], sb_q_norm_g[j], sb_w_out[j])
        x = x + (1.0 + mod[:, 1, 2])[:, None, :] * y
        h = modulate(rms_norm(x, norm_g[l, 2]), mod[:, 2, 0], mod[:, 2, 1])
        x = x + FFN_HALF * (1.0 + mod[:, 2, 2])[:, None, :] * swiglu(h, ffn_w_gu[l, 1], ffn_w_down[l, 1])
        if l == N_A_LAYERS - 1:
            k_sh, v_sh = shared_kv(x, c_act, kv_ada_w, kv_ada_b, kv_norm_g, kv_w, k_norm_g)
    return x
```

```python
import functools

import jax
import jax.numpy as jnp
from jax import lax
from jax.experimental import pallas as pl
from jax.experimental.pallas import tpu as pltpu

F32 = jnp.float32
BF16 = jnp.bfloat16
EPS = 1e-6
LANES = 128
SUBLANES = 8
GDN_CHUNK = 64
VMEM_LIMIT = 56 * 1024 * 1024
HIGHEST = lax.Precision.HIGHEST


def _pick(n, candidates):
    for c in candidates:
        if c <= n and n % c == 0:
            return c
    return n


def _params(sem):
    return pltpu.CompilerParams(dimension_semantics=sem, vmem_limit_bytes=VMEM_LIMIT)


def _silu(x):
    return x * jax.nn.sigmoid(x)


def _softplus(x):
    return jnp.maximum(x, 0.0) + jnp.log1p(jnp.exp(-jnp.abs(x)))


def _dot(a, b, precision=None):
    return jnp.dot(a, b, preferred_element_type=F32, precision=precision)


def _dot_nt(a, b, precision=None):
    return lax.dot_general(a, b, (((1,), (1,)), ((), ())), preferred_element_type=F32,
                           precision=precision)


def _dot_tn(a, b, precision=None):
    return lax.dot_general(a, b, (((0,), (0,)), ((), ())), preferred_element_type=F32,
                           precision=precision)


def _norm_mod(x, g, shift, scale):
    ms = jnp.mean(x * x, axis=-1, keepdims=True)
    y = x * lax.rsqrt(ms + EPS) * g
    return y * (1.0 + scale) + shift


def _ada_kernel(c_ref, w_ref, b_ref, o_ref):
    ca = _silu(c_ref[...])
    o_ref[0] = _dot(ca, w_ref[0]) + b_ref[0]


def _ada(c8, w, b):
    nl, d, n = w.shape
    tn = _pick(n, (1024, 512, 256, 128))
    return pl.pallas_call(
        _ada_kernel,
        grid=(nl, n // tn),
        in_specs=[pl.BlockSpec((SUBLANES, d), lambda l, j: (0, 0)),
                  pl.BlockSpec((1, d, tn), lambda l, j: (l, 0, j)),
                  pl.BlockSpec((1, 1, tn), lambda l, j: (l, 0, j))],
        out_specs=pl.BlockSpec((1, SUBLANES, tn), lambda l, j: (l, 0, j)),
        out_shape=jax.ShapeDtypeStruct((nl, SUBLANES, n), F32),
        compiler_params=_params(("parallel", "parallel")),
        name="ada_proj",
    )(c8, w, b.reshape(nl, 1, n))


def _ffn_kernel(x_ref, mod_ref, g_ref, wg_ref, wu_ref, wd_ref, o_ref, h_ref, acc_ref):
    j = pl.program_id(1)

    @pl.when(j == 0)
    def _():
        h = _norm_mod(x_ref[...], g_ref[...], mod_ref[0, 0:1, :], mod_ref[0, 1:2, :])
        h_ref[...] = h.astype(BF16)
        acc_ref[...] = jnp.zeros_like(acc_ref)

    h = h_ref[...]
    gate = _dot(h, wg_ref[...])
    up = _dot(h, wu_ref[...])
    a = (_silu(gate) * up).astype(BF16)
    acc_ref[...] += _dot(a, wd_ref[...])

    @pl.when(j == pl.num_programs(1) - 1)
    def _():
        o_ref[...] = x_ref[...] + 0.5 * (1.0 + mod_ref[0, 2:3, :]) * acc_ref[...]


def _ffn(x, mod, g, w_gu, w_down, seq):
    t, d = x.shape
    f = w_down.shape[0]
    tm = _pick(seq, (512, 256, 128, 64, 32, 16, 8))
    tf = _pick(f, (512, 256, 128))
    tpb = seq // tm
    nf = f // tf
    return pl.pallas_call(
        _ffn_kernel,
        grid=(t // tm, nf),
        in_specs=[pl.BlockSpec((tm, d), lambda i, j: (i, 0)),
                  pl.BlockSpec((1, 3, d), lambda i, j: (i // tpb, 0, 0)),
                  pl.BlockSpec((1, d), lambda i, j: (0, 0)),
                  pl.BlockSpec((d, tf), lambda i, j: (0, j)),
                  pl.BlockSpec((d, tf), lambda i, j: (0, j + nf)),
                  pl.BlockSpec((tf, d), lambda i, j: (j, 0))],
        out_specs=pl.BlockSpec((tm, d), lambda i, j: (i, 0)),
        out_shape=jax.ShapeDtypeStruct((t, d), F32),
        scratch_shapes=[pltpu.VMEM((tm, d), BF16), pltpu.VMEM((tm, d), F32)],
        compiler_params=_params(("parallel", "arbitrary")),
        name="ffn",
    )(x, mod, g.reshape(1, d), w_gu, w_gu, w_down)


def _proj_kernel(x_ref, mod_ref, g_ref, w_ref, hg_ref, o_ref, h_ref, *, head_dim, out_scale):
    j = pl.program_id(1)

    @pl.when(j == 0)
    def _():
        h = _norm_mod(x_ref[...], g_ref[...], mod_ref[0, 0:1, :], mod_ref[0, 1:2, :])
        h_ref[...] = h.astype(BF16)

    y = _dot(h_ref[...], w_ref[...])
    if head_dim is None:
        o_ref[...] = y.astype(o_ref.dtype)
    else:
        for s in range(0, y.shape[1], head_dim):
            yh = y[:, s:s + head_dim]
            ms = jnp.mean(yh * yh, axis=-1, keepdims=True)
            yn = yh * lax.rsqrt(ms + EPS) * hg_ref[:, s:s + head_dim]
            o_ref[:, s:s + head_dim] = (yn * out_scale).astype(o_ref.dtype)


def _proj(x, mod, g, w, seq, out_dtype, head_gain=None, out_scale=1.0):
    t, d = x.shape
    n = w.shape[1]
    tm = _pick(seq, (512, 256, 128, 64, 32, 16, 8))
    tn = _pick(n, (512, 256, 128))
    tpb = seq // tm
    if head_gain is None:
        head_dim = None
        hg = jnp.zeros((1, tn), F32)
    else:
        head_dim = head_gain.shape[0]
        hg = jnp.tile(head_gain.astype(F32), tn // head_dim).reshape(1, tn)
    return pl.pallas_call(
        functools.partial(_proj_kernel, head_dim=head_dim, out_scale=out_scale),
        grid=(t // tm, n // tn),
        in_specs=[pl.BlockSpec((tm, d), lambda i, j: (i, 0)),
                  pl.BlockSpec((1, 3, d), lambda i, j: (i // tpb, 0, 0)),
                  pl.BlockSpec((1, d), lambda i, j: (0, 0)),
                  pl.BlockSpec((d, tn), lambda i, j: (0, j)),
                  pl.BlockSpec((1, tn), lambda i, j: (0, 0))],
        out_specs=pl.BlockSpec((tm, tn), lambda i, j: (i, j)),
        out_shape=jax.ShapeDtypeStruct((t, n), out_dtype),
        scratch_shapes=[pltpu.VMEM((tm, d), BF16)],
        compiler_params=_params(("parallel", "arbitrary")),
        name="norm_proj",
    )(x, mod, g.reshape(1, d), w, hg)


def _out_proj_kernel(a_ref, w_ref, x_ref, mod_ref, o_ref):
    y = _dot(a_ref[...], w_ref[...])
    o_ref[...] = x_ref[...] + (1.0 + mod_ref[0, 2:3, :]) * y


def _out_proj(a, w, x, mod, seq):
    t, k = a.shape
    d = w.shape[1]
    tm = _pick(seq, (512, 256, 128, 64, 32, 16, 8))
    tn = _pick(d, (512, 256, 128))
    tpb = seq // tm
    return pl.pallas_call(
        _out_proj_kernel,
        grid=(t // tm, d // tn),
        in_specs=[pl.BlockSpec((tm, k), lambda i, j: (i, 0)),
                  pl.BlockSpec((k, tn), lambda i, j: (0, j)),
                  pl.BlockSpec((tm, tn), lambda i, j: (i, j)),
                  pl.BlockSpec((1, 3, tn), lambda i, j: (i // tpb, 0, j))],
        out_specs=pl.BlockSpec((tm, tn), lambda i, j: (i, j)),
        out_shape=jax.ShapeDtypeStruct((t, d), F32),
        compiler_params=_params(("parallel", "arbitrary")),
        name="out_proj",
    )(a, w, x, mod)


def _conv_kernel(prev_ref, x_ref, w_ref, o_ref, *, tiles_per_batch, n_q_tiles, n_qk_tiles,
                 head_dim, q_scale):
    i = pl.program_id(0)
    j = pl.program_id(1)
    tm = x_ref.shape[0]
    kw = w_ref.shape[0]
    first = (i % tiles_per_batch) == 0
    prev = jnp.where(first, 0.0, prev_ref[...])
    xx = jnp.concatenate([prev, x_ref[...]], axis=0)
    y = None
    for tap in range(kw):
        off = SUBLANES - (kw - 1) + tap
        term = xx[off:off + tm] * w_ref[tap:tap + 1, :]
        y = term if y is None else y + term
    y = _silu(y)

    @pl.when(j >= n_qk_tiles)
    def _():
        o_ref[...] = y

    @pl.when(j < n_qk_tiles)
    def _():
        scale = jnp.where(j < n_q_tiles, q_scale, 1.0)
        for s in range(0, y.shape[1], head_dim):
            yh = y[:, s:s + head_dim]
            ss = jnp.sum(yh * yh, axis=-1, keepdims=True)
            o_ref[:, s:s + head_dim] = yh * (lax.rsqrt(ss + EPS) * scale)


def _gdn_conv(proj, conv_w, seq, qk_dim, head_dim):
    t = proj.shape[0]
    kw, c = conv_w.shape
    tm = _pick(seq, (512, 256, 128, 64, 32, 16, 8))
    tc = _pick(qk_dim, (512, 256, 128))
    tpb = seq // tm
    rb = tm // SUBLANES
    kern = functools.partial(_conv_kernel, tiles_per_batch=tpb, n_q_tiles=qk_dim // tc,
                             n_qk_tiles=2 * qk_dim // tc, head_dim=head_dim,
                             q_scale=float(head_dim) ** -0.5)
    return pl.pallas_call(
        kern,
        grid=(t // tm, c // tc),
        in_specs=[pl.BlockSpec((SUBLANES, tc), lambda i, j: (jnp.maximum(i * rb - 1, 0), j)),
                  pl.BlockSpec((tm, tc), lambda i, j: (i, j)),
                  pl.BlockSpec((kw, tc), lambda i, j: (0, j))],
        out_specs=pl.BlockSpec((tm, tc), lambda i, j: (i, j)),
        out_shape=jax.ShapeDtypeStruct((t, c), F32),
        compiler_params=_params(("parallel", "parallel")),
        name="gdn_conv",
    )(proj, proj, conv_w)


def _unit_lower_inverse(m, eye):
    c = m.shape[0]
    t = eye - m
    p = m
    k = 2
    while k < c:
        p = _dot(p, p, HIGHEST)
        t = t + _dot(t, p, HIGHEST)
        k *= 2
    return t


def _gdn_kernel(q_ref, k_ref, v_ref, z_ref, ba_ref, at_ref, alr_ref, dtr_ref, alc_ref, dtc_ref,
                on_ref, o_ref, s_ref, *, n_heads, heads_per_block):
    jb = pl.program_id(1)
    n = pl.program_id(2)
    tc = q_ref.shape[0]
    hd = q_ref.shape[1]
    cs = GDN_CHUNK
    hb = heads_per_block

    @pl.when(n == 0)
    def _():
        s_ref[...] = jnp.zeros_like(s_ref)

    ri = lax.broadcasted_iota(jnp.int32, (tc, tc), 0)
    ci = lax.broadcasted_iota(jnp.int32, (tc, tc), 1)
    shift = cs.bit_length() - 1
    same = (ri >> shift) == (ci >> shift)
    lblk = jnp.where(same & (ri >= ci), 1.0, 0.0).astype(F32)
    ublk = jnp.where(same & (ri <= ci), 1.0, 0.0).astype(F32)

    ba = ba_ref[...]
    beta_all = jax.nn.sigmoid(ba)
    g_all = -jnp.exp(alr_ref[...]) * _softplus(ba + dtr_ref[...])
    gc_all = _dot(lblk, g_all, HIGHEST)
    g_row = -jnp.exp(alc_ref[0]) * _softplus(at_ref[0, 0] + dtc_ref[0])
    gc_row = _dot(g_row, ublk, HIGHEST)

    lane = lax.broadcasted_iota(jnp.int32, (cs, LANES), 1)
    ii = lax.broadcasted_iota(jnp.int32, (cs, cs), 0)
    jj = lax.broadcasted_iota(jnp.int32, (cs, cs), 1)
    incl = ii >= jj
    strict = ii > jj
    eye = jnp.where(ii == jj, 1.0, 0.0).astype(F32)
    on_g = on_ref[...]

    for c in range(tc // cs):
        r0 = c * cs
        qc = q_ref[r0:r0 + cs, :]
        kc = k_ref[r0:r0 + cs, :]
        kk = _dot_nt(kc, kc, HIGHEST)
        qk = _dot_nt(qc, kc, HIGHEST)
        for hh in range(hb):
            h = jb * hb + hh
            bcol = jnp.sum(jnp.where(lane == h, beta_all[r0:r0 + cs], 0.0), axis=1, keepdims=True)
            gcol = jnp.sum(jnp.where(lane == n_heads + h, gc_all[r0:r0 + cs], 0.0), axis=1,
                           keepdims=True)
            grow = gc_row[hh:hh + 1, r0:r0 + cs]
            glast = grow[:, cs - 1:cs]
            dec = jnp.where(incl, jnp.exp(jnp.where(incl, gcol - grow, 0.0)), 0.0)
            m = bcol * kk * jnp.where(strict, dec, 0.0)
            tinv = _unit_lower_inverse(m, eye)
            vc = v_ref[r0:r0 + cs, hh * hd:(hh + 1) * hd]
            eg = jnp.exp(gcol)
            rhs = jnp.concatenate([vc * bcol, kc * (bcol * eg)], axis=1)
            uw = _dot(tinv, rhs, HIGHEST)
            u = uw[:, :hd]
            w = uw[:, hd:]
            attn = qk * dec
            qg = qc * eg
            kd = kc * jnp.exp(glast - gcol)
            st = s_ref[hh]
            v_new = u - _dot(w, st, HIGHEST)
            o = _dot(qg, st, HIGHEST) + _dot(attn, v_new, HIGHEST)
            s_ref[hh] = st * jnp.exp(glast) + _dot_tn(kd, v_new, HIGHEST)
            ms = jnp.mean(o * o, axis=-1, keepdims=True)
            zc = z_ref[r0:r0 + cs, hh * hd:(hh + 1) * hd]
            out = o * lax.rsqrt(ms + EPS) * on_g * _silu(zc)
            o_ref[r0:r0 + cs, hh * hd:(hh + 1) * hd] = out.astype(o_ref.dtype)


def _gdn_core(qkv, proj, ba, a_t, a_log, dt_bias, onorm_g, batch, seq, qk_dim, v_dim, n_heads,
              head_dim, z_col0):
    t = qkv.shape[0]
    hd = head_dim
    qk_heads = qk_dim // hd
    rep = n_heads // qk_heads
    hb = rep
    nblk = n_heads // hb
    tc = _pick(seq, (256, 128, 64))
    nt = seq // tc
    pad = jnp.zeros((LANES - 2 * n_heads,), F32)
    zeros_h = jnp.zeros((n_heads,), F32)
    alr = jnp.concatenate([zeros_h, a_log.astype(F32), pad]).reshape(1, LANES)
    dtr = jnp.concatenate([zeros_h, dt_bias.astype(F32), pad]).reshape(1, LANES)
    alc = a_log.astype(F32).reshape(nblk, hb, 1)
    dtc = dt_bias.astype(F32).reshape(nblk, hb, 1)
    a_t = a_t.reshape(batch, nblk, hb, seq)
    kern = functools.partial(_gdn_kernel, n_heads=n_heads, heads_per_block=hb)
    k_off = qk_dim // hd
    v_off = 2 * qk_dim // (hb * hd)
    z_off = z_col0 // (hb * hd)
    return pl.pallas_call(
        kern,
        grid=(batch, nblk, nt),
        in_specs=[pl.BlockSpec((tc, hd), lambda b, j, n: (b * nt + n, j)),
                  pl.BlockSpec((tc, hd), lambda b, j, n: (b * nt + n, k_off + j)),
                  pl.BlockSpec((tc, hb * hd), lambda b, j, n: (b * nt + n, v_off + j)),
                  pl.BlockSpec((tc, hb * hd), lambda b, j, n: (b * nt + n, z_off + j)),
                  pl.BlockSpec((tc, LANES), lambda b, j, n: (b * nt + n, 0)),
                  pl.BlockSpec((1, 1, hb, tc), lambda b, j, n: (b, j, 0, n)),
                  pl.BlockSpec((1, LANES), lambda b, j, n: (0, 0)),
                  pl.BlockSpec((1, LANES), lambda b, j, n: (0, 0)),
                  pl.BlockSpec((1, hb, 1), lambda b, j, n: (j, 0, 0)),
                  pl.BlockSpec((1, hb, 1), lambda b, j, n: (j, 0, 0)),
                  pl.BlockSpec((1, hd), lambda b, j, n: (0, 0))],
        out_specs=pl.BlockSpec((tc, hb * hd), lambda b, j, n: (b * nt + n, j)),
        out_shape=jax.ShapeDtypeStruct((t, v_dim), BF16),
        scratch_shapes=[pltpu.VMEM((hb, hd, hd), F32)],
        compiler_params=_params(("parallel", "parallel", "arbitrary")),
        name="gdn_delta_rule",
    )(qkv, qkv, qkv, proj, ba, a_t, alr, dtr, alc, dtc, onorm_g.astype(F32).reshape(1, hd))


def _sb_kernel(q_ref, k_ref, v_ref, o_ref, acc_ref, lr_ref, *, tk):
    i = pl.program_id(2)
    tq = q_ref.shape[1]
    nd = tq // tk

    acc_ref[...] = jnp.zeros_like(acc_ref)
    lr_ref[...] = jnp.zeros_like(lr_ref)

    ri = lax.broadcasted_iota(jnp.int32, (2 * tk, tk + LANES), 0) & (tk - 1)
    ci = lax.broadcasted_iota(jnp.int32, (2 * tk, tk + LANES), 1)
    ucat = jnp.where((ci >= tk) | (ri > ci), 1.0, 0.0).astype(BF16)

    def tile(key_start, r0, masked):
        rows = tq - r0
        q = q_ref[0, r0:tq, :]
        kt = k_ref[0, pl.ds(key_start, tk), :]
        vt = v_ref[0, pl.ds(key_start, tk), :]
        z = _dot_nt(q, kt)
        lse = jnp.log1p(jnp.exp(-jnp.abs(z)))
        log_beta = jnp.minimum(z, 0.0) - lse
        log_1mb = log_beta - z
        if masked:
            qi = lax.broadcasted_iota(jnp.int32, (rows, tk), 0)
            ki = lax.broadcasted_iota(jnp.int32, (rows, tk), 1)
            mask = ki < qi
            log_1mb = jnp.where(mask, log_1mb, 0.0)
        hi = log_1mb.astype(BF16)
        lo = (log_1mb - hi.astype(F32)).astype(BF16)
        sums = _dot(jnp.concatenate([hi, lo], axis=1), ucat)
        right = sums[:, :tk]
        total = sums[:, tk:]
        lr = lr_ref[r0:tq, :]
        wts = jnp.exp(log_beta + right + lr)
        if masked:
            wts = jnp.where(mask, wts, 0.0)
        acc_ref[r0:tq, :] += _dot(wts.astype(BF16), vt)
        lr_ref[r0:tq, :] = lr + total

    base = pl.multiple_of(i * tq, tq)
    for dt in range(nd - 1, -1, -1):
        tile(pl.multiple_of(base + dt * tk, tk), dt * tk, True)

    n_full = i * nd

    def body(t, carry):
        kt_idx = n_full - 1 - t
        tile(pl.multiple_of(kt_idx * tk, tk), 0, False)
        return carry

    lax.fori_loop(0, n_full, body, 0)
    o_ref[0] = acc_ref[...].astype(o_ref.dtype)


def _sb_attention(q, k, v, head_dim):
    b, s, d = q.shape
    assert head_dim == LANES
    nh = d // head_dim
    tq = _pick(s, (512, 256, 128))
    tk = LANES
    return pl.pallas_call(
        functools.partial(_sb_kernel, tk=tk),
        grid=(b, nh, s // tq),
        in_specs=[pl.BlockSpec((1, tq, head_dim), lambda b_, h, i: (b_, i, h)),
                  pl.BlockSpec((1, s, head_dim), lambda b_, h, i: (b_, 0, h)),
                  pl.BlockSpec((1, s, head_dim), lambda b_, h, i: (b_, 0, h))],
        out_specs=pl.BlockSpec((1, tq, head_dim), lambda b_, h, i: (b_, i, h)),
        out_shape=jax.ShapeDtypeStruct((b, s, d), BF16),
        scratch_shapes=[pltpu.VMEM((tq, head_dim), F32), pltpu.VMEM((tq, LANES), F32)],
        compiler_params=_params(("parallel", "parallel", "arbitrary")),
        name="sb_attention",
    )(q, k, v)


def kernel(x, c, ada_w, ada_b, norm_g, ffn_w_gu, ffn_w_down, gdn_w_in, gdn_conv_w, gdn_a_log,
           gdn_dt_bias, gdn_onorm_g, gdn_w_out, kv_ada_w, kv_ada_b, kv_norm_g, kv_w, k_norm_g,
           sb_w_q, sb_q_norm_g, sb_w_out):
    batch, seq, d = x.shape
    depth = ada_w.shape[0]
    n_a = gdn_w_in.shape[0]
    n_heads = gdn_a_log.shape[1]
    hd = gdn_onorm_g.shape[1]
    v_dim = n_heads * hd
    conv_ch = gdn_conv_w.shape[2]
    qk_dim = (conv_ch - v_dim) // 2
    sb_hd = k_norm_g.shape[0]
    t = batch * seq
    assert batch <= SUBLANES and 2 * n_heads <= LANES

    c8 = jnp.pad(c.astype(F32), ((0, SUBLANES - batch), (0, 0)))
    mod_all = _ada(c8, ada_w, ada_b)[:, :batch].reshape(depth, batch, 3, 3, d)
    kv_mod = _ada(c8, kv_ada_w[None], kv_ada_b[None])[0, :batch].reshape(batch, 2, d)
    kv_mod = jnp.concatenate([kv_mod, jnp.zeros((batch, 1, d), F32)], axis=1)

    xf = x.reshape(t, d)
    k_sh = v_sh = None
    for l in range(depth):
        xf = _ffn(xf, mod_all[l, :, 0], norm_g[l, 0], ffn_w_gu[l, 0].astype(BF16),
                  ffn_w_down[l, 0].astype(BF16), seq)
        mod = mod_all[l, :, 1]
        if l < n_a:
            w_in = gdn_w_in[l]
            n_main = conv_ch + v_dim
            proj = _proj(xf, mod, norm_g[l, 1], w_in[:, :n_main].astype(BF16), seq, F32)
            w_ba = jnp.pad(w_in[:, n_main:], ((0, 0), (0, LANES - 2 * n_heads))).astype(BF16)
            ba = _proj(xf, mod, norm_g[l, 1], w_ba, seq, F32)
            qkv = _gdn_conv(proj, gdn_conv_w[l], seq, qk_dim, hd)
            a_t = jnp.transpose(ba.reshape(batch, seq, LANES)[:, :, n_heads:2 * n_heads], (0, 2, 1))
            og = _gdn_core(qkv, proj, ba, a_t, gdn_a_log[l], gdn_dt_bias[l], gdn_onorm_g[l],
                           batch, seq, qk_dim, v_dim, n_heads, hd, conv_ch)
            xf = _out_proj(og, gdn_w_out[l].astype(BF16), xf, mod, seq)
        else:
            jl = l - n_a
            q = _proj(xf, mod, norm_g[l, 1], sb_w_q[jl].astype(BF16), seq, BF16,
                      head_gain=sb_q_norm_g[jl], out_scale=float(sb_hd) ** -0.5)
            o = _sb_attention(q.reshape(batch, seq, d), k_sh, v_sh, sb_hd)
            xf = _out_proj(o.reshape(t, d), sb_w_out[jl].astype(BF16), xf, mod, seq)
        xf = _ffn(xf, mod_all[l, :, 2], norm_g[l, 2], ffn_w_gu[l, 1].astype(BF16),
                  ffn_w_down[l, 1].astype(BF16), seq)
        if l == n_a - 1:
            kv_wb = kv_w.astype(BF16)
            k_sh = _proj(xf, kv_mod, kv_norm_g, kv_wb[:, :d], seq, BF16,
                         head_gain=k_norm_g).reshape(batch, seq, d)
            v_sh = _proj(xf, kv_mod, kv_norm_g, kv_wb[:, d:], seq, BF16).reshape(batch, seq, d)
    return xf.reshape(batch, seq, d)
```

```python
import functools

import jax
import jax.numpy as jnp
from jax import lax
from jax.experimental import pallas as pl
from jax.experimental.pallas import tpu as pltpu

F32 = jnp.float32
BF16 = jnp.bfloat16
EPS = 1e-6
LANES = 128
SUBLANES = 8
GDN_CHUNK = 64
VMEM_LIMIT = 56 * 1024 * 1024
HIGHEST = lax.Precision.HIGHEST
LOG2E = 1.4426950408889634


def _pick(n, candidates):
    for c in candidates:
        if c <= n and n % c == 0:
            return c
    return n


def _params(sem):
    return pltpu.CompilerParams(dimension_semantics=sem, vmem_limit_bytes=VMEM_LIMIT)


def _silu(x):
    return x * jax.nn.sigmoid(x)


def _softplus(x):
    return jnp.maximum(x, 0.0) + jnp.log1p(jnp.exp(-jnp.abs(x)))


def _dot(a, b, precision=None):
    return jnp.dot(a, b, preferred_element_type=F32, precision=precision)


def _dot_nt(a, b, precision=None):
    return lax.dot_general(a, b, (((1,), (1,)), ((), ())), preferred_element_type=F32,
                           precision=precision)


def _norm_mod(x, g, shift, scale):
    ms = jnp.mean(x * x, axis=-1, keepdims=True)
    y = x * lax.rsqrt(ms + EPS) * g
    return y * (1.0 + scale) + shift


def _ada_kernel(c_ref, w_ref, b_ref, o_ref):
    ca = _silu(c_ref[...])
    o_ref[0] = _dot(ca, w_ref[0]) + b_ref[0]


def _ada(c8, w, b):
    nl, d, n = w.shape
    tn = _pick(n, (1024, 512, 256, 128))
    return pl.pallas_call(
        _ada_kernel,
        grid=(nl, n // tn),
        in_specs=[pl.BlockSpec((SUBLANES, d), lambda l, j: (0, 0)),
                  pl.BlockSpec((1, d, tn), lambda l, j: (l, 0, j)),
                  pl.BlockSpec((1, 1, tn), lambda l, j: (l, 0, j))],
        out_specs=pl.BlockSpec((1, SUBLANES, tn), lambda l, j: (l, 0, j)),
        out_shape=jax.ShapeDtypeStruct((nl, SUBLANES, n), F32),
        compiler_params=_params(("parallel", "parallel")),
        name="ada_proj",
    )(c8, w, b.reshape(nl, 1, n))


def _ffn_kernel(x_ref, mod_ref, g_ref, wg_ref, wu_ref, wd_ref, o_ref, h_ref, acc_ref):
    j = pl.program_id(1)

    @pl.when(j == 0)
    def _():
        h = _norm_mod(x_ref[...], g_ref[...], mod_ref[0, 0:1, :], mod_ref[0, 1:2, :])
        h_ref[...] = h.astype(BF16)
        acc_ref[...] = jnp.zeros_like(acc_ref)

    h = h_ref[...]
    gate = _dot(h, wg_ref[...])
    up = _dot(h, wu_ref[...])
    a = (_silu(gate) * up).astype(BF16)
    acc_ref[...] += _dot(a, wd_ref[...])

    @pl.when(j == pl.num_programs(1) - 1)
    def _():
        o_ref[...] = x_ref[...] + 0.5 * (1.0 + mod_ref[0, 2:3, :]) * acc_ref[...]


def _ffn(x, mod, g, w_gu, w_down, seq):
    t, d = x.shape
    f = w_down.shape[0]
    tm = _pick(seq, (512, 256, 128, 64, 32, 16, 8))
    tf = _pick(f, (512, 256, 128))
    tpb = seq // tm
    nf = f // tf
    return pl.pallas_call(
        _ffn_kernel,
        grid=(t // tm, nf),
        in_specs=[pl.BlockSpec((tm, d), lambda i, j: (i, 0)),
                  pl.BlockSpec((1, 3, d), lambda i, j: (i // tpb, 0, 0)),
                  pl.BlockSpec((1, d), lambda i, j: (0, 0)),
                  pl.BlockSpec((d, tf), lambda i, j: (0, j)),
                  pl.BlockSpec((d, tf), lambda i, j: (0, j + nf)),
                  pl.BlockSpec((tf, d), lambda i, j: (j, 0))],
        out_specs=pl.BlockSpec((tm, d), lambda i, j: (i, 0)),
        out_shape=jax.ShapeDtypeStruct((t, d), F32),
        scratch_shapes=[pltpu.VMEM((tm, d), BF16), pltpu.VMEM((tm, d), F32)],
        compiler_params=_params(("parallel", "arbitrary")),
        name="ffn",
    )(x, mod, g.reshape(1, d), w_gu, w_gu, w_down)


def _proj_kernel(x_ref, mod_ref, g_ref, w_ref, hg_ref, o_ref, h_ref, *, head_dim, out_scale):
    j = pl.program_id(1)

    @pl.when(j == 0)
    def _():
        h = _norm_mod(x_ref[...], g_ref[...], mod_ref[0, 0:1, :], mod_ref[0, 1:2, :])
        h_ref[...] = h.astype(BF16)

    y = _dot(h_ref[...], w_ref[...])
    if head_dim is None:
        o_ref[...] = y.astype(o_ref.dtype)
    else:
        for s in range(0, y.shape[1], head_dim):
            yh = y[:, s:s + head_dim]
            ms = jnp.mean(yh * yh, axis=-1, keepdims=True)
            yn = yh * lax.rsqrt(ms + EPS) * hg_ref[:, s:s + head_dim]
            o_ref[:, s:s + head_dim] = (yn * out_scale).astype(o_ref.dtype)


def _proj(x, mod, g, w, seq, out_dtype, head_gain=None, out_scale=1.0):
    t, d = x.shape
    n = w.shape[1]
    tm = _pick(seq, (512, 256, 128, 64, 32, 16, 8))
    tn = _pick(n, (512, 256, 128))
    tpb = seq // tm
    if head_gain is None:
        head_dim = None
        hg = jnp.zeros((1, tn), F32)
    else:
        head_dim = head_gain.shape[0]
        hg = jnp.tile(head_gain.astype(F32), tn // head_dim).reshape(1, tn)
    return pl.pallas_call(
        functools.partial(_proj_kernel, head_dim=head_dim, out_scale=out_scale),
        grid=(t // tm, n // tn),
        in_specs=[pl.BlockSpec((tm, d), lambda i, j: (i, 0)),
                  pl.BlockSpec((1, 3, d), lambda i, j: (i // tpb, 0, 0)),
                  pl.BlockSpec((1, d), lambda i, j: (0, 0)),
                  pl.BlockSpec((d, tn), lambda i, j: (0, j)),
                  pl.BlockSpec((1, tn), lambda i, j: (0, 0))],
        out_specs=pl.BlockSpec((tm, tn), lambda i, j: (i, j)),
        out_shape=jax.ShapeDtypeStruct((t, n), out_dtype),
        scratch_shapes=[pltpu.VMEM((tm, d), BF16)],
        compiler_params=_params(("parallel", "arbitrary")),
        name="norm_proj",
    )(x, mod, g.reshape(1, d), w, hg)


def _out_proj_kernel(a_ref, w_ref, x_ref, mod_ref, o_ref):
    y = _dot(a_ref[...], w_ref[...])
    o_ref[...] = x_ref[...] + (1.0 + mod_ref[0, 2:3, :]) * y


def _out_proj(a, w, x, mod, seq):
    t, k = a.shape
    d = w.shape[1]
    tm = _pick(seq, (512, 256, 128, 64, 32, 16, 8))
    tn = _pick(d, (512, 256, 128))
    tpb = seq // tm
    return pl.pallas_call(
        _out_proj_kernel,
        grid=(t // tm, d // tn),
        in_specs=[pl.BlockSpec((tm, k), lambda i, j: (i, 0)),
                  pl.BlockSpec((k, tn), lambda i, j: (0, j)),
                  pl.BlockSpec((tm, tn), lambda i, j: (i, j)),
                  pl.BlockSpec((1, 3, tn), lambda i, j: (i // tpb, 0, j))],
        out_specs=pl.BlockSpec((tm, tn), lambda i, j: (i, j)),
        out_shape=jax.ShapeDtypeStruct((t, d), F32),
        compiler_params=_params(("parallel", "arbitrary")),
        name="out_proj",
    )(a, w, x, mod)


def _conv_kernel(prev_ref, x_ref, w_ref, o_ref, *, tiles_per_batch, n_q_tiles, n_qk_tiles,
                 head_dim, q_scale):
    i = pl.program_id(0)
    j = pl.program_id(1)
    tm = x_ref.shape[0]
    kw = w_ref.shape[0]
    first = (i % tiles_per_batch) == 0
    prev = jnp.where(first, 0.0, prev_ref[...])
    xx = jnp.concatenate([prev, x_ref[...]], axis=0)
    y = None
    for tap in range(kw):
        off = SUBLANES - (kw - 1) + tap
        term = xx[off:off + tm] * w_ref[tap:tap + 1, :]
        y = term if y is None else y + term
    y = _silu(y)

    @pl.when(j >= n_qk_tiles)
    def _():
        o_ref[...] = y

    @pl.when(j < n_qk_tiles)
    def _():
        scale = jnp.where(j < n_q_tiles, q_scale, 1.0)
        for s in range(0, y.shape[1], head_dim):
            yh = y[:, s:s + head_dim]
            ss = jnp.sum(yh * yh, axis=-1, keepdims=True)
            o_ref[:, s:s + head_dim] = yh * (lax.rsqrt(ss + EPS) * scale)


def _gdn_conv(proj, conv_w, seq, qk_dim, head_dim):
    t = proj.shape[0]
    kw, c = conv_w.shape
    tm = _pick(seq, (512, 256, 128, 64, 32, 16, 8))
    tc = _pick(qk_dim, (512, 256, 128))
    tpb = seq // tm
    rb = tm // SUBLANES
    kern = functools.partial(_conv_kernel, tiles_per_batch=tpb, n_q_tiles=qk_dim // tc,
                             n_qk_tiles=2 * qk_dim // tc, head_dim=head_dim,
                             q_scale=float(head_dim) ** -0.5)
    return pl.pallas_call(
        kern,
        grid=(t // tm, c // tc),
        in_specs=[pl.BlockSpec((SUBLANES, tc), lambda i, j: (jnp.maximum(i * rb - 1, 0), j)),
                  pl.BlockSpec((tm, tc), lambda i, j: (i, j)),
                  pl.BlockSpec((kw, tc), lambda i, j: (0, j))],
        out_specs=pl.BlockSpec((tm, tc), lambda i, j: (i, j)),
        out_shape=jax.ShapeDtypeStruct((t, c), F32),
        compiler_params=_params(("parallel", "parallel")),
        name="gdn_conv",
    )(proj, proj, conv_w)


def _split_hl(a):
    hi = a.astype(BF16)
    lo = (a - hi.astype(F32)).astype(BF16)
    return hi, lo


def _gdn_kernel(q_ref, k_ref, kt_ref, v_ref, z_ref, ba_ref, at_ref, alr_ref, dtr_ref, alc_ref,
                dtc_ref, on_ref, o_ref, s_ref, *, n_heads):
    jb = pl.program_id(1)
    n = pl.program_id(2)
    tc = q_ref.shape[0]
    hd = q_ref.shape[1]
    cs = GDN_CHUNK
    nc = tc // cs

    @pl.when(n == 0)
    def _():
        s_ref[...] = jnp.zeros_like(s_ref)

    ri = lax.broadcasted_iota(jnp.int32, (tc, tc), 0)
    ci = lax.broadcasted_iota(jnp.int32, (tc, tc), 1)
    shift = cs.bit_length() - 1
    same = (ri >> shift) == (ci >> shift)
    lblk = jnp.where(same & (ri >= ci), 1.0, 0.0).astype(F32)
    ublk = jnp.where(same & (ri <= ci), 1.0, 0.0).astype(F32)

    ba = ba_ref[...]
    beta_all = jax.nn.sigmoid(ba)
    g_all = -jnp.exp(alr_ref[...]) * _softplus(ba + dtr_ref[...])
    gc_all = _dot(lblk, g_all, HIGHEST)
    g_row = -jnp.exp(alc_ref[0]) * _softplus(at_ref[0, 0] + dtc_ref[0])
    gc_row = _dot(g_row, ublk, HIGHEST)

    lane = lax.broadcasted_iota(jnp.int32, (cs, 2 * cs), 1)
    left = lane < cs
    lane2 = lax.broadcasted_iota(jnp.int32, (cs, 2 * hd), 1)
    left2 = lane2 < hd
    left_s = lax.broadcasted_iota(jnp.int32, (hd, 2 * hd), 1) < hd
    ii = lax.broadcasted_iota(jnp.int32, (cs, 2 * cs), 0)
    jj = lane & (cs - 1)
    incl = ii >= jj
    strict = ii > jj
    eye = jnp.where(ii == jj, 1.0, 0.0).astype(F32)
    zb = jnp.zeros((cs, hd), BF16)
    on_g = on_ref[...]
    h0 = jb * 2

    def col(x, idx):
        return jnp.sum(jnp.where(lane == idx, x, 0.0), axis=1, keepdims=True)

    def bd(x):
        zero = jnp.zeros_like(x)
        return jnp.concatenate([jnp.where(left, x, zero), jnp.where(left, zero, x)], axis=0)

    def bd_wide(x):
        zero = jnp.zeros((x.shape[0], hd), x.dtype)
        return jnp.concatenate([jnp.concatenate([x[:, :hd], zero], axis=1),
                                jnp.concatenate([zero, x[:, hd:]], axis=1)], axis=0)

    chunks = []
    for c in range(nc):
        r0 = c * cs
        b0 = col(beta_all[r0:r0 + cs], h0)
        b1 = col(beta_all[r0:r0 + cs], h0 + 1)
        g0 = col(gc_all[r0:r0 + cs], n_heads + h0)
        g1 = col(gc_all[r0:r0 + cs], n_heads + h0 + 1)
        grow = jnp.concatenate([gc_row[0:1, r0:r0 + cs], gc_row[1:2, r0:r0 + cs]], axis=1)
        glast = jnp.where(left[0:1], gc_row[0:1, r0 + cs - 1:r0 + cs],
                          gc_row[1:2, r0 + cs - 1:r0 + cs])
        gcol = jnp.where(left, g0, g1)
        bcol = jnp.where(left, b0, b1)
        dec = jnp.where(incl, jnp.exp(jnp.where(incl, gcol - grow, 0.0)), 0.0)
        qc = q_ref[r0:r0 + cs, :]
        kc = k_ref[r0:r0 + cs, :]
        kcb = kc.astype(BF16)
        qkk = _dot_nt(jnp.concatenate([qc.astype(BF16), kcb], axis=0),
                      jnp.concatenate([kcb, kcb], axis=0))
        attn = qkk[:cs] * dec
        m = bcol * qkk[cs:] * jnp.where(strict, dec, 0.0)
        b2 = jnp.where(left2, b0, b1)
        eg2 = jnp.where(left2, jnp.exp(g0), jnp.exp(g1))
        vb = (v_ref[r0:r0 + cs, :] * b2).astype(BF16)
        kbe = (jnp.concatenate([kc, kc], axis=1) * (b2 * eg2)).astype(BF16)
        rmat = jnp.concatenate(
            [jnp.concatenate([vb[:, :hd], zb, kbe[:, :hd], zb], axis=1),
             jnp.concatenate([zb, vb[:, hd:], zb, kbe[:, hd:]], axis=1)], axis=0)
        qg = jnp.concatenate([qc, qc], axis=1) * eg2
        ktc = kt_ref[:, r0:r0 + cs]
        kdt = jnp.concatenate([ktc, ktc], axis=1) * jnp.exp(glast - grow)
        gl2 = jnp.where(left_s[0:1], jnp.exp(glast[:, 0:1]), jnp.exp(glast[:, cs:cs + 1]))
        chunks.append(dict(m=m, attn=attn, rmat=rmat, qg=qg, kdt=kdt, gl2=gl2))

    ts = [eye - ch["m"] for ch in chunks]
    ps = [_split_hl(ch["m"]) for ch in chunks]
    k = 2
    while k < cs:
        p2s = [_dot(jnp.concatenate([ph, plo, ph], axis=1),
                    jnp.concatenate([bd(ph), bd(ph), bd(plo)], axis=0)) for ph, plo in ps]
        ps = [_split_hl(p2) for p2 in p2s]
        new_ts = []
        for (p2h, p2l), t in zip(ps, ts):
            th, tl = _split_hl(t)
            rhs2 = jnp.concatenate([bd(p2h), bd(p2h), bd(p2l)], axis=0)
            new_ts.append(t + _dot(jnp.concatenate([th, tl, th], axis=1), rhs2))
        ts = new_ts
        k *= 2

    uws = []
    for ch, t in zip(chunks, ts):
        th, tl = _split_hl(t)
        uws.append(_dot(jnp.concatenate([th, tl], axis=1),
                        jnp.concatenate([ch["rmat"], ch["rmat"]], axis=0)))

    for c, (ch, uw) in enumerate(zip(chunks, uws)):
        r0 = c * cs
        st = s_ref[...]
        u = uw[:, :2 * hd]
        w = uw[:, 2 * hd:]
        r1 = _dot(jnp.concatenate([w, ch["qg"]], axis=0).astype(BF16), bd_wide(st.astype(BF16)))
        v_new = u - r1[:cs]
        r2 = _dot(jnp.concatenate([ch["attn"], ch["kdt"]], axis=0).astype(BF16),
                  bd_wide(v_new.astype(BF16)))
        o = r1[cs:] + r2[:cs]
        s_ref[...] = st * ch["gl2"] + r2[cs:]
        for hh in range(2):
            oh = o[:, hh * hd:(hh + 1) * hd]
            ms = jnp.mean(oh * oh, axis=-1, keepdims=True)
            zc = z_ref[r0:r0 + cs, hh * hd:(hh + 1) * hd]
            out = oh * lax.rsqrt(ms + EPS) * on_g * _silu(zc)
            o_ref[r0:r0 + cs, hh * hd:(hh + 1) * hd] = out.astype(o_ref.dtype)


def _gdn_core(qkv, k_t, proj, ba, a_t, a_log, dt_bias, onorm_g, batch, seq, qk_dim, v_dim, n_heads,
              head_dim, z_col0):
    t = qkv.shape[0]
    hd = head_dim
    qk_heads = qk_dim // hd
    hb = n_heads // qk_heads
    assert hb == 2 and 2 * GDN_CHUNK == LANES and hd == LANES
    nblk = n_heads // hb
    tc = _pick(seq, (256, 128, 64))
    nt = seq // tc
    pad = jnp.zeros((LANES - 2 * n_heads,), F32)
    zeros_h = jnp.zeros((n_heads,), F32)
    alr = jnp.concatenate([zeros_h, a_log.astype(F32), pad]).reshape(1, LANES)
    dtr = jnp.concatenate([zeros_h, dt_bias.astype(F32), pad]).reshape(1, LANES)
    alc = a_log.astype(F32).reshape(nblk, hb, 1)
    dtc = dt_bias.astype(F32).reshape(nblk, hb, 1)
    a_t = a_t.reshape(batch, nblk, hb, seq)
    kern = functools.partial(_gdn_kernel, n_heads=n_heads)
    k_off = qk_dim // hd
    v_off = 2 * qk_dim // (hb * hd)
    z_off = z_col0 // (hb * hd)
    return pl.pallas_call(
        kern,
        grid=(batch, nblk, nt),
        in_specs=[pl.BlockSpec((tc, hd), lambda b, j, n: (b * nt + n, j)),
                  pl.BlockSpec((tc, hd), lambda b, j, n: (b * nt + n, k_off + j)),
                  pl.BlockSpec((hd, tc), lambda b, j, n: (j, b * nt + n)),
                  pl.BlockSpec((tc, hb * hd), lambda b, j, n: (b * nt + n, v_off + j)),
                  pl.BlockSpec((tc, hb * hd), lambda b, j, n: (b * nt + n, z_off + j)),
                  pl.BlockSpec((tc, LANES), lambda b, j, n: (b * nt + n, 0)),
                  pl.BlockSpec((1, 1, hb, tc), lambda b, j, n: (b, j, 0, n)),
                  pl.BlockSpec((1, LANES), lambda b, j, n: (0, 0)),
                  pl.BlockSpec((1, LANES), lambda b, j, n: (0, 0)),
                  pl.BlockSpec((1, hb, 1), lambda b, j, n: (j, 0, 0)),
                  pl.BlockSpec((1, hb, 1), lambda b, j, n: (j, 0, 0)),
                  pl.BlockSpec((1, hd), lambda b, j, n: (0, 0))],
        out_specs=pl.BlockSpec((tc, hb * hd), lambda b, j, n: (b * nt + n, j)),
        out_shape=jax.ShapeDtypeStruct((t, v_dim), BF16),
        scratch_shapes=[pltpu.VMEM((hd, hb * hd), F32)],
        compiler_params=_params(("parallel", "parallel", "arbitrary")),
        name="gdn_delta_rule",
    )(qkv, qkv, k_t, qkv, proj, ba, a_t, alr, dtr, alc, dtc, onorm_g.astype(F32).reshape(1, hd))


def _sb_kernel(q_ref, k_ref, v_ref, o_ref, acc_ref, lr_ref, zn_ref, wt_ref, *, tk):
    i = pl.program_id(2)
    tq = q_ref.shape[1]
    ks = 2 * tk
    nd = tq // ks

    acc_ref[...] = jnp.zeros_like(acc_ref)
    lr_ref[...] = jnp.zeros_like(lr_ref)

    ri = lax.broadcasted_iota(jnp.int32, (2 * tk, tk + LANES), 0) & (tk - 1)
    ci = lax.broadcasted_iota(jnp.int32, (2 * tk, tk + LANES), 1)
    ucat = jnp.where((ci >= tk) | (ri >= ci), 1.0, 0.0).astype(BF16)

    def scores(q, start):
        return _dot_nt(q, k_ref[0, pl.ds(start, ks), :])

    def log_parts(zn, mask):
        nabs = lax.bitcast_convert_type(
            lax.bitcast_convert_type(zn, jnp.uint32) | jnp.uint32(0x80000000), F32)
        l1 = jnp.minimum(zn, 0.0) - jnp.log2(1.0 + jnp.exp2(nabs))
        if mask is not None:
            l1 = jnp.where(mask, l1, 0.0)
        hi_f = lax.bitcast_convert_type(
            lax.bitcast_convert_type(l1, jnp.uint32) & jnp.uint32(0xFFFF0000), F32)
        hi = hi_f.astype(BF16)
        lo = (l1 - hi_f).astype(BF16)
        return (jnp.concatenate([hi[:, :tk], lo[:, :tk]], axis=1),
                jnp.concatenate([hi[:, tk:], lo[:, tk:]], axis=1))

    def suffix_sums(parts):
        return _dot(parts[0], ucat), _dot(parts[1], ucat)

    def exp_weights(zns, sums, lr, mask):
        wts = []
        for zn, (sa, sb) in zip(zns, sums):
            lra = lr + sb[:, tk:]
            wb = jnp.exp2(sb[:, :tk] + lr - zn[:, tk:])
            wa = jnp.exp2(sa[:, :tk] + lra - zn[:, :tk])
            w = jnp.concatenate([wa, wb], axis=1)
            if mask is not None:
                w = jnp.where(mask, w, 0.0)
            wts.append(w.astype(BF16))
            lr = lra + sa[:, tk:]
        return wts, lr

    def weights(zns, lr, mask):
        sums = [suffix_sums(log_parts(zn, mask)) for zn in zns]
        return exp_weights(zns, sums, lr, mask)

    base = pl.multiple_of(i * tq, tq)
    for dt in range(nd - 1, -1, -1):
        r0 = dt * ks
        rows = tq - r0
        start = pl.multiple_of(base + r0, ks)
        qi = lax.broadcasted_iota(jnp.int32, (rows, ks), 0)
        ki = lax.broadcasted_iota(jnp.int32, (rows, ks), 1)
        zn = scores(q_ref[0, r0:tq, :], start)
        wts, lr = weights([zn], lr_ref[r0:tq, :], ki < qi)
        acc_ref[r0:tq, :] += _dot(wts[0], v_ref[0, pl.ds(start, ks), :])
        lr_ref[r0:tq, :] = lr

    def window(t, u):
        tile = jnp.maximum(i - 1 - t, 0)
        return pl.multiple_of(tile * tq + (nd - 1 - u) * ks, ks)

    q = q_ref[0]
    for u in range(nd):
        zn_ref[u] = scores(q, window(0, u))
    wt_ref[...] = jnp.zeros_like(wt_ref)

    def body(t, carry):
        zns = [zn_ref[u] for u in range(nd)]
        parts = [log_parts(zn, None) for zn in zns]
        pv = None
        for u in range(nd):
            term = _dot(wt_ref[u], v_ref[0, pl.ds(window(jnp.maximum(t - 1, 0), u), ks), :])
            pv = term if pv is None else pv + term
        acc_ref[...] += pv
        nxt = [scores(q, window(t + 1, u)) for u in range(nd)]
        sums = [suffix_sums(p) for p in parts]
        wts, lr = exp_weights(zns, sums, lr_ref[...], None)
        lr_ref[...] = lr
        for u in range(nd):
            wt_ref[u] = wts[u]
            zn_ref[u] = nxt[u]
        return carry

    lax.fori_loop(0, i, body, 0)
    pv = None
    for u in range(nd):
        term = _dot(wt_ref[u], v_ref[0, pl.ds(window(jnp.maximum(i - 1, 0), u), ks), :])
        pv = term if pv is None else pv + term
    o_ref[0] = (acc_ref[...] + pv).astype(o_ref.dtype)


def _sb_attention(q, k, v, head_dim):
    b, s, d = q.shape
    assert head_dim == LANES
    nh = d // head_dim
    tk = LANES
    tq = _pick(s, (512, 256))
    assert tq % (2 * tk) == 0
    return pl.pallas_call(
        functools.partial(_sb_kernel, tk=tk),
        grid=(b, nh, s // tq),
        in_specs=[pl.BlockSpec((1, tq, head_dim), lambda b_, h, i: (b_, i, h)),
                  pl.BlockSpec((1, s, head_dim), lambda b_, h, i: (b_, 0, h)),
                  pl.BlockSpec((1, s, head_dim), lambda b_, h, i: (b_, 0, h))],
        out_specs=pl.BlockSpec((1, tq, head_dim), lambda b_, h, i: (b_, i, h)),
        out_shape=jax.ShapeDtypeStruct((b, s, d), BF16),
        scratch_shapes=[pltpu.VMEM((tq, head_dim), F32), pltpu.VMEM((tq, LANES), F32),
                        pltpu.VMEM((tq // (2 * tk), tq, 2 * tk), F32),
                        pltpu.VMEM((tq // (2 * tk), tq, 2 * tk), BF16)],
        compiler_params=_params(("parallel", "parallel", "arbitrary")),
        name="sb_attention",
    )(q, k, v)


def kernel(x, c, ada_w, ada_b, norm_g, ffn_w_gu, ffn_w_down, gdn_w_in, gdn_conv_w, gdn_a_log,
           gdn_dt_bias, gdn_onorm_g, gdn_w_out, kv_ada_w, kv_ada_b, kv_norm_g, kv_w, k_norm_g,
           sb_w_q, sb_q_norm_g, sb_w_out):
    batch, seq, d = x.shape
    depth = ada_w.shape[0]
    n_a = gdn_w_in.shape[0]
    n_heads = gdn_a_log.shape[1]
    hd = gdn_onorm_g.shape[1]
    v_dim = n_heads * hd
    conv_ch = gdn_conv_w.shape[2]
    qk_dim = (conv_ch - v_dim) // 2
    sb_hd = k_norm_g.shape[0]
    t = batch * seq
    assert batch <= SUBLANES and 2 * n_heads <= LANES

    c8 = jnp.pad(c.astype(F32), ((0, SUBLANES - batch), (0, 0)))
    mod_all = _ada(c8, ada_w, ada_b)[:, :batch].reshape(depth, batch, 3, 3, d)
    kv_mod = _ada(c8, kv_ada_w[None], kv_ada_b[None])[0, :batch].reshape(batch, 2, d)
    kv_mod = jnp.concatenate([kv_mod, jnp.zeros((batch, 1, d), F32)], axis=1)

    xf = x.reshape(t, d)
    k_sh = v_sh = None
    for l in range(depth):
        xf = _ffn(xf, mod_all[l, :, 0], norm_g[l, 0], ffn_w_gu[l, 0].astype(BF16),
                  ffn_w_down[l, 0].astype(BF16), seq)
        mod = mod_all[l, :, 1]
        if l < n_a:
            w_in = gdn_w_in[l]
            n_main = conv_ch + v_dim
            proj = _proj(xf, mod, norm_g[l, 1], w_in[:, :n_main].astype(BF16), seq, F32)
            w_ba = jnp.pad(w_in[:, n_main:], ((0, 0), (0, LANES - 2 * n_heads))).astype(BF16)
            ba = _proj(xf, mod, norm_g[l, 1], w_ba, seq, F32)
            qkv = _gdn_conv(proj, gdn_conv_w[l], seq, qk_dim, hd)
            a_t = jnp.transpose(ba.reshape(batch, seq, LANES)[:, :, n_heads:2 * n_heads], (0, 2, 1))
            k_t = jnp.transpose(qkv[:, qk_dim:2 * qk_dim])
            og = _gdn_core(qkv, k_t, proj, ba, a_t, gdn_a_log[l], gdn_dt_bias[l], gdn_onorm_g[l],
                           batch, seq, qk_dim, v_dim, n_heads, hd, conv_ch)
            xf = _out_proj(og, gdn_w_out[l].astype(BF16), xf, mod, seq)
        else:
            jl = l - n_a
            q = _proj(xf, mod, norm_g[l, 1], sb_w_q[jl].astype(BF16), seq, BF16,
                      head_gain=sb_q_norm_g[jl], out_scale=-LOG2E * float(sb_hd) ** -0.5)
            o = _sb_attention(q.reshape(batch, seq, d), k_sh, v_sh, sb_hd)
            xf = _out_proj(o.reshape(t, d), sb_w_out[jl].astype(BF16), xf, mod, seq)
        xf = _ffn(xf, mod_all[l, :, 2], norm_g[l, 2], ffn_w_gu[l, 1].astype(BF16),
                  ffn_w_down[l, 1].astype(BF16), seq)
        if l == n_a - 1:
            kv_wb = kv_w.astype(BF16)
            k_sh = _proj(xf, kv_mod, kv_norm_g, kv_wb[:, :d], seq, BF16,
                         head_gain=k_norm_g).reshape(batch, seq, d)
            v_sh = _proj(xf, kv_mod, kv_norm_g, kv_wb[:, d:], seq, BF16).reshape(batch, seq, d)
    return xf.reshape(batch, seq, d)
```

```python
import functools

import jax
import jax.numpy as jnp
from jax import lax
from jax.experimental import pallas as pl
from jax.experimental.pallas import tpu as pltpu

F32 = jnp.float32
BF16 = jnp.bfloat16
EPS = 1e-6
LANES = 128
SUBLANES = 8
GDN_CHUNK = 64
VMEM_LIMIT = 56 * 1024 * 1024
LOG2E = 1.4426950408889634


def _pick(n, candidates):
    for c in candidates:
        if c <= n and n % c == 0:
            return c
    return n


def _params(sem):
    return pltpu.CompilerParams(dimension_semantics=sem, vmem_limit_bytes=VMEM_LIMIT)


def _silu(x):
    return x * jax.nn.sigmoid(x)


def _softplus(x):
    return jnp.maximum(x, 0.0) + jnp.log1p(jnp.exp(-jnp.abs(x)))


def _dot(a, b):
    return jnp.dot(a, b, preferred_element_type=F32)


def _dot_nt(a, b):
    return lax.dot_general(a, b, (((1,), (1,)), ((), ())), preferred_element_type=F32)


def _norm_mod(x, g, shift, scale):
    ms = jnp.mean(x * x, axis=-1, keepdims=True)
    y = x * lax.rsqrt(ms + EPS) * g
    return y * (1.0 + scale) + shift


def _ada_kernel(c_ref, w_ref, b_ref, o_ref):
    ca = _silu(c_ref[...])
    o_ref[0] = _dot(ca, w_ref[0]) + b_ref[0]


def _ada(c8, w, b):
    nl, d, n = w.shape
    tn = _pick(n, (1024, 512, 256, 128))
    return pl.pallas_call(
        _ada_kernel,
        grid=(nl, n // tn),
        in_specs=[pl.BlockSpec((SUBLANES, d), lambda l, j: (0, 0)),
                  pl.BlockSpec((1, d, tn), lambda l, j: (l, 0, j)),
                  pl.BlockSpec((1, 1, tn), lambda l, j: (l, 0, j))],
        out_specs=pl.BlockSpec((1, SUBLANES, tn), lambda l, j: (l, 0, j)),
        out_shape=jax.ShapeDtypeStruct((nl, SUBLANES, n), F32),
        compiler_params=_params(("parallel", "parallel")),
        name="ada_proj",
    )(c8, w, b.reshape(nl, 1, n))


def _ffn_kernel(x_ref, mod_ref, g_ref, wg_ref, wu_ref, wd_ref, o_ref, h_ref, acc_ref):
    j = pl.program_id(1)

    @pl.when(j == 0)
    def _():
        h = _norm_mod(x_ref[...], g_ref[...], mod_ref[0, 0:1, :], mod_ref[0, 1:2, :])
        h_ref[...] = h.astype(BF16)
        acc_ref[...] = jnp.zeros_like(acc_ref)

    h = h_ref[...]
    gate = _dot(h, wg_ref[...])
    up = _dot(h, wu_ref[...])
    a = (_silu(gate) * up).astype(BF16)
    acc_ref[...] += _dot(a, wd_ref[...])

    @pl.when(j == pl.num_programs(1) - 1)
    def _():
        o_ref[...] = x_ref[...] + 0.5 * (1.0 + mod_ref[0, 2:3, :]) * acc_ref[...]


def _ffn(x, mod, g, w_gu, w_down, seq):
    t, d = x.shape
    f = w_down.shape[0]
    tm = _pick(seq, (512, 256, 128, 64, 32, 16, 8))
    tf = _pick(f, (512, 256, 128))
    tpb = seq // tm
    nf = f // tf
    return pl.pallas_call(
        _ffn_kernel,
        grid=(t // tm, nf),
        in_specs=[pl.BlockSpec((tm, d), lambda i, j: (i, 0)),
                  pl.BlockSpec((1, 3, d), lambda i, j: (i // tpb, 0, 0)),
                  pl.BlockSpec((1, d), lambda i, j: (0, 0)),
                  pl.BlockSpec((d, tf), lambda i, j: (0, j)),
                  pl.BlockSpec((d, tf), lambda i, j: (0, j + nf)),
                  pl.BlockSpec((tf, d), lambda i, j: (j, 0))],
        out_specs=pl.BlockSpec((tm, d), lambda i, j: (i, 0)),
        out_shape=jax.ShapeDtypeStruct((t, d), F32),
        scratch_shapes=[pltpu.VMEM((tm, d), BF16), pltpu.VMEM((tm, d), F32)],
        compiler_params=_params(("parallel", "arbitrary")),
        name="ffn",
    )(x, mod, g.reshape(1, d), w_gu, w_gu, w_down)


def _proj_kernel(*refs, head_dim, out_scale, n_norm_tiles, has_extra):
    if has_extra:
        x_ref, mod_ref, g_ref, w_ref, hg_ref, wx_ref, o_ref, ox_ref, h_ref = refs
    else:
        x_ref, mod_ref, g_ref, w_ref, hg_ref, o_ref, h_ref = refs
    j = pl.program_id(1)

    @pl.when(j == 0)
    def _():
        h = _norm_mod(x_ref[...], g_ref[...], mod_ref[0, 0:1, :], mod_ref[0, 1:2, :])
        h_ref[...] = h.astype(BF16)
        if has_extra:
            ox_ref[...] = _dot(h_ref[...], wx_ref[...])

    y = _dot(h_ref[...], w_ref[...])

    def plain():
        o_ref[...] = y.astype(o_ref.dtype)

    def head_normed():
        for s in range(0, y.shape[1], head_dim):
            yh = y[:, s:s + head_dim]
            ms = jnp.mean(yh * yh, axis=-1, keepdims=True)
            yn = yh * lax.rsqrt(ms + EPS) * hg_ref[:, s:s + head_dim]
            o_ref[:, s:s + head_dim] = (yn * out_scale).astype(o_ref.dtype)

    if head_dim is None:
        plain()
    elif n_norm_tiles is None:
        head_normed()
    else:
        pl.when(j < n_norm_tiles)(head_normed)
        pl.when(j >= n_norm_tiles)(plain)


def _proj(x, mod, g, w, seq, out_dtype, head_gain=None, out_scale=1.0, norm_cols=None, w_extra=None):
    t, d = x.shape
    n = w.shape[1]
    tm = _pick(seq, (512, 256, 128, 64, 32, 16, 8))
    tn = _pick(n if norm_cols is None else norm_cols, (1024, 512, 256, 128))
    assert n % tn == 0
    tpb = seq // tm
    if head_gain is None:
        head_dim = None
        hg = jnp.zeros((1, tn), F32)
    else:
        head_dim = head_gain.shape[0]
        hg = jnp.tile(head_gain.astype(F32), tn // head_dim).reshape(1, tn)
    n_norm_tiles = None if norm_cols is None else norm_cols // tn
    has_extra = w_extra is not None
    in_specs = [pl.BlockSpec((tm, d), lambda i, j: (i, 0)),
                pl.BlockSpec((1, 3, d), lambda i, j: (i // tpb, 0, 0)),
                pl.BlockSpec((1, d), lambda i, j: (0, 0)),
                pl.BlockSpec((d, tn), lambda i, j: (0, j)),
                pl.BlockSpec((1, tn), lambda i, j: (0, 0))]
    out_specs = pl.BlockSpec((tm, tn), lambda i, j: (i, j))
    out_shape = jax.ShapeDtypeStruct((t, n), out_dtype)
    args = [x, mod, g.reshape(1, d), w, hg]
    if has_extra:
        in_specs.append(pl.BlockSpec((d, LANES), lambda i, j: (0, 0)))
        out_specs = [out_specs, pl.BlockSpec((tm, LANES), lambda i, j: (i, 0))]
        out_shape = [out_shape, jax.ShapeDtypeStruct((t, LANES), F32)]
        args.append(w_extra)
    return pl.pallas_call(
        functools.partial(_proj_kernel, head_dim=head_dim, out_scale=out_scale,
                          n_norm_tiles=n_norm_tiles, has_extra=has_extra),
        grid=(t // tm, n // tn),
        in_specs=in_specs,
        out_specs=out_specs,
        out_shape=out_shape,
        scratch_shapes=[pltpu.VMEM((tm, d), BF16)],
        compiler_params=_params(("parallel", "arbitrary")),
        name="norm_proj",
    )(*args)


def _out_proj_kernel(a_ref, w_ref, x_ref, mod_ref, o_ref):
    y = _dot(a_ref[...], w_ref[...])
    o_ref[...] = x_ref[...] + (1.0 + mod_ref[0, 2:3, :]) * y


def _out_proj(a, w, x, mod, seq):
    t, k = a.shape
    d = w.shape[1]
    tm = _pick(seq, (512, 256, 128, 64, 32, 16, 8))
    tn = _pick(d, (1024, 512, 256, 128))
    tpb = seq // tm
    return pl.pallas_call(
        _out_proj_kernel,
        grid=(t // tm, d // tn),
        in_specs=[pl.BlockSpec((tm, k), lambda i, j: (i, 0)),
                  pl.BlockSpec((k, tn), lambda i, j: (0, j)),
                  pl.BlockSpec((tm, tn), lambda i, j: (i, j)),
                  pl.BlockSpec((1, 3, tn), lambda i, j: (i // tpb, 0, j))],
        out_specs=pl.BlockSpec((tm, tn), lambda i, j: (i, j)),
        out_shape=jax.ShapeDtypeStruct((t, d), F32),
        compiler_params=_params(("parallel", "arbitrary")),
        name="out_proj",
    )(a, w, x, mod)


def _conv_kernel(prev_ref, x_ref, w_ref, o_ref, xx_ref, *, tiles_per_batch, n_q_tiles, n_qk_tiles,
                 head_dim, q_scale):
    i = pl.program_id(0)
    j = pl.program_id(1)
    tm = x_ref.shape[0]
    kw = w_ref.shape[0]
    first = (i % tiles_per_batch) == 0
    xx_ref[0:SUBLANES, :] = jnp.where(first, 0.0, prev_ref[...])
    xx_ref[SUBLANES:SUBLANES + tm, :] = x_ref[...]
    y = None
    for tap in range(kw):
        off = SUBLANES - (kw - 1) + tap
        term = xx_ref[off:off + tm, :] * w_ref[tap:tap + 1, :]
        y = term if y is None else y + term
    y = _silu(y)

    @pl.when(j >= n_qk_tiles)
    def _():
        o_ref[...] = y

    @pl.when(j < n_qk_tiles)
    def _():
        scale = jnp.where(j < n_q_tiles, q_scale, 1.0)
        for s in range(0, y.shape[1], head_dim):
            yh = y[:, s:s + head_dim]
            ss = jnp.sum(yh * yh, axis=-1, keepdims=True)
            o_ref[:, s:s + head_dim] = yh * (lax.rsqrt(ss + EPS) * scale)


def _gdn_conv(proj, conv_w, seq, qk_dim, head_dim):
    t = proj.shape[0]
    kw, c = conv_w.shape
    tm = _pick(seq, (512, 256, 128, 64, 32, 16, 8))
    tc = _pick(qk_dim, (512, 256, 128))
    tpb = seq // tm
    rb = tm // SUBLANES
    kern = functools.partial(_conv_kernel, tiles_per_batch=tpb, n_q_tiles=qk_dim // tc,
                             n_qk_tiles=2 * qk_dim // tc, head_dim=head_dim,
                             q_scale=float(head_dim) ** -0.5)
    return pl.pallas_call(
        kern,
        grid=(t // tm, c // tc),
        in_specs=[pl.BlockSpec((SUBLANES, tc), lambda i, j: (jnp.maximum(i * rb - 1, 0), j)),
                  pl.BlockSpec((tm, tc), lambda i, j: (i, j)),
                  pl.BlockSpec((kw, tc), lambda i, j: (0, j))],
        out_specs=pl.BlockSpec((tm, tc), lambda i, j: (i, j)),
        out_shape=jax.ShapeDtypeStruct((t, c), F32),
        scratch_shapes=[pltpu.VMEM((tm + SUBLANES, tc), F32)],
        compiler_params=_params(("parallel", "parallel")),
        name="gdn_conv",
    )(proj, proj, conv_w)


def _split_hl(a):
    hi = a.astype(BF16)
    lo = (a - hi.astype(F32)).astype(BF16)
    return hi, lo


def _split3(a):
    h1 = a.astype(BF16)
    r1 = a - h1.astype(F32)
    h2 = r1.astype(BF16)
    h3 = (r1 - h2.astype(F32)).astype(BF16)
    return h1, h2, h3


def _gdn_kernel(q_ref, k_ref, kt_ref, v_ref, z_ref, ba_ref, at_ref, alr_ref, dtr_ref, alc_ref,
                dtc_ref, on_ref, o_ref, s_ref, cq_ref, bm_ref, om_ref, gl_ref, *, n_heads):
    jb = pl.program_id(1)
    n = pl.program_id(2)
    tc = q_ref.shape[0]
    hd = q_ref.shape[1]
    cs = GDN_CHUNK
    nc = tc // cs
    wslot = n % 2
    rslot = 1 - wslot

    @pl.when(n == 0)
    def _():
        s_ref[...] = jnp.zeros_like(s_ref)
        cq_ref[1] = jnp.zeros_like(cq_ref[1])
        bm_ref[1] = jnp.zeros_like(bm_ref[1])
        om_ref[1] = jnp.zeros_like(om_ref[1])
        gl_ref[1] = jnp.zeros_like(gl_ref[1])

    lane = lax.broadcasted_iota(jnp.int32, (cs, 2 * cs), 1)
    left = lane < cs
    lane2 = lax.broadcasted_iota(jnp.int32, (cs, 2 * hd), 1)
    left2 = lane2 < hd
    left_s = lax.broadcasted_iota(jnp.int32, (1, 2 * hd), 1) < hd
    ii = lax.broadcasted_iota(jnp.int32, (cs, 2 * cs), 0)
    jj = lane & (cs - 1)
    incl = ii >= jj
    strict = ii > jj
    eye = jnp.where(ii == jj, 1.0, 0.0).astype(F32)
    zb = jnp.zeros((cs, hd), BF16)
    on_g = on_ref[...]
    h0 = jb * 2
    si = lax.broadcasted_iota(jnp.int32, (cs, 3 * cs), 0)
    sj = lax.broadcasted_iota(jnp.int32, (cs, 3 * cs), 1) % cs
    low3 = jnp.where(si >= sj, 1.0, 0.0).astype(BF16)
    ui = lax.broadcasted_iota(jnp.int32, (3 * cs, cs), 0) % cs
    uj = lax.broadcasted_iota(jnp.int32, (3 * cs, cs), 1)
    up3 = jnp.where(ui <= uj, 1.0, 0.0).astype(BF16)

    def col(x, idx):
        return jnp.sum(jnp.where(lane == idx, x, 0.0), axis=1, keepdims=True)

    def bd(x):
        zero = jnp.zeros_like(x)
        return jnp.concatenate([jnp.where(left, x, zero), jnp.where(left, zero, x)], axis=0)

    def bd_wide(x):
        zero = jnp.zeros((x.shape[0], hd), x.dtype)
        return jnp.concatenate([jnp.concatenate([x[:, :hd], zero], axis=1),
                                jnp.concatenate([zero, x[:, hd:]], axis=1)], axis=0)

    def recurrence(c):
        st = s_ref[...]
        r = _dot(cq_ref[rslot, c], bd_wide(st.astype(BF16)))
        s_ref[...] = st * gl_ref[rslot, c] + bm_ref[rslot, c] - r[:hd]
        o = r[hd:] + om_ref[rslot, c]
        r0 = c * cs
        for hh in range(2):
            oh = o[:, hh * hd:(hh + 1) * hd]
            ms = jnp.mean(oh * oh, axis=-1, keepdims=True)
            zc = z_ref[r0:r0 + cs, hh * hd:(hh + 1) * hd]
            out = oh * lax.rsqrt(ms + EPS) * on_g * _silu(zc)
            o_ref[r0:r0 + cs, hh * hd:(hh + 1) * hd] = out.astype(o_ref.dtype)

    pending = list(range(nc))

    def run_recurrence(k):
        for _ in range(k):
            if pending:
                recurrence(pending.pop(0))

    ba = ba_ref[...]
    beta_all = jax.nn.sigmoid(ba)
    g_all = -jnp.exp(alr_ref[...]) * _softplus(ba + dtr_ref[...])
    g_row = -jnp.exp(alc_ref[0]) * _softplus(at_ref[0, 0] + dtc_ref[0])
    gcs, grs = [], []
    for c in range(nc):
        r0 = c * cs
        gcs.append(_dot(low3, jnp.concatenate(_split3(g_all[r0:r0 + cs]), axis=0)))
        grs.append(_dot(jnp.concatenate(_split3(g_row[:, r0:r0 + cs]), axis=1), up3))
    run_recurrence(1)

    chunks = []
    for c in range(nc):
        r0 = c * cs
        b0 = col(beta_all[r0:r0 + cs], h0)
        b1 = col(beta_all[r0:r0 + cs], h0 + 1)
        g0 = col(gcs[c], n_heads + h0)
        g1 = col(gcs[c], n_heads + h0 + 1)
        grow = jnp.concatenate([grs[c][0:1], grs[c][1:2]], axis=1)
        glast = jnp.where(left[0:1], grs[c][0:1, cs - 1:cs], grs[c][1:2, cs - 1:cs])
        gcol = jnp.where(left, g0, g1)
        bcol = jnp.where(left, b0, b1)
        dec = jnp.where(incl, jnp.exp(jnp.where(incl, gcol - grow, 0.0)), 0.0)
        qc = q_ref[r0:r0 + cs, :]
        kc = k_ref[r0:r0 + cs, :]
        kcb = kc.astype(BF16)
        qkk = _dot_nt(jnp.concatenate([qc.astype(BF16), kcb], axis=0),
                      jnp.concatenate([kcb, kcb], axis=0))
        attn = qkk[:cs] * dec
        m = bcol * qkk[cs:] * jnp.where(strict, dec, 0.0)
        b2 = jnp.where(left2, b0, b1)
        eg2 = jnp.where(left2, jnp.exp(g0), jnp.exp(g1))
        vb = (v_ref[r0:r0 + cs, :] * b2).astype(BF16)
        kbe = (jnp.concatenate([kc, kc], axis=1) * (b2 * eg2)).astype(BF16)
        rmat = jnp.concatenate(
            [jnp.concatenate([vb[:, :hd], zb, kbe[:, :hd], zb], axis=1),
             jnp.concatenate([zb, vb[:, hd:], zb, kbe[:, hd:]], axis=1)], axis=0)
        qg = jnp.concatenate([qc, qc], axis=1) * eg2
        ktc = kt_ref[:, r0:r0 + cs]
        kdt = jnp.concatenate([ktc, ktc], axis=1) * jnp.exp(glast - grow)
        gl_ref[wslot, c] = jnp.where(left_s, jnp.exp(glast[:, 0:1]), jnp.exp(glast[:, cs:cs + 1]))
        chunks.append(dict(m=m, ak=jnp.concatenate([attn, kdt], axis=0).astype(BF16),
                           rmat=rmat, qg=qg))

    ts = [eye - ch["m"] for ch in chunks]
    ps = [_split_hl(ch["m"]) for ch in chunks]
    k = 2
    while k < cs:
        p2s = [_dot(jnp.concatenate([ph, plo, ph], axis=1),
                    jnp.concatenate([bd(ph), bd(ph), bd(plo)], axis=0)) for ph, plo in ps]
        ps = [_split_hl(p2) for p2 in p2s]
        if k in (4, 16):
            run_recurrence(1)
        new_ts = []
        for (p2h, p2l), t in zip(ps, ts):
            th, tl = _split_hl(t)
            rhs2 = jnp.concatenate([bd(p2h), bd(p2h), bd(p2l)], axis=0)
            new_ts.append(t + _dot(jnp.concatenate([th, tl, th], axis=1), rhs2))
        ts = new_ts
        k *= 2
    run_recurrence(1)

    uws = []
    for ch, t in zip(chunks, ts):
        th, tl = _split_hl(t)
        uws.append(_dot(jnp.concatenate([th, tl], axis=1),
                        jnp.concatenate([ch["rmat"], ch["rmat"]], axis=0)))
    run_recurrence(nc)

    for c, (ch, uw) in enumerate(zip(chunks, uws)):
        uwb = uw.astype(BF16)
        x = _dot(ch["ak"], jnp.concatenate([bd_wide(uwb[:, :2 * hd]), bd_wide(uwb[:, 2 * hd:])],
                                           axis=1))
        om_ref[wslot, c] = x[:cs, :2 * hd]
        bm_ref[wslot, c] = x[cs:, :2 * hd]
        cq_ref[wslot, c] = jnp.concatenate([x[cs:, 2 * hd:], ch["qg"] - x[:cs, 2 * hd:]],
                                           axis=0).astype(BF16)


def _gdn_core(qkv, k_t, proj, ba, a_t, a_log, dt_bias, onorm_g, batch, seq, qk_dim, v_dim, n_heads,
              head_dim, z_col0):
    t = qkv.shape[0]
    hd = head_dim
    qk_heads = qk_dim // hd
    hb = n_heads // qk_heads
    cs = GDN_CHUNK
    assert hb == 2 and 2 * cs == LANES and hd == LANES
    nblk = n_heads // hb
    tc = _pick(seq, (256, 128, 64))
    nt = seq // tc
    nc = tc // cs
    pad = jnp.zeros((LANES - 2 * n_heads,), F32)
    zeros_h = jnp.zeros((n_heads,), F32)
    alr = jnp.concatenate([zeros_h, a_log.astype(F32), pad]).reshape(1, LANES)
    dtr = jnp.concatenate([zeros_h, dt_bias.astype(F32), pad]).reshape(1, LANES)
    alc = a_log.astype(F32).reshape(nblk, hb, 1)
    dtc = dt_bias.astype(F32).reshape(nblk, hb, 1)
    a_t = a_t.reshape(batch, nblk, hb, seq)
    kern = functools.partial(_gdn_kernel, n_heads=n_heads)
    k_off = qk_dim // hd
    v_off = 2 * qk_dim // (hb * hd)
    z_off = z_col0 // (hb * hd)

    def cur(b, n):
        return b * nt + jnp.minimum(n, nt - 1)

    def prev(b, n):
        return b * nt + jnp.maximum(n - 1, 0)

    return pl.pallas_call(
        kern,
        grid=(batch, nblk, nt + 1),
        in_specs=[pl.BlockSpec((tc, hd), lambda b, j, n: (cur(b, n), j)),
                  pl.BlockSpec((tc, hd), lambda b, j, n: (cur(b, n), k_off + j)),
                  pl.BlockSpec((hd, tc), lambda b, j, n: (j, cur(b, n))),
                  pl.BlockSpec((tc, hb * hd), lambda b, j, n: (cur(b, n), v_off + j)),
                  pl.BlockSpec((tc, hb * hd), lambda b, j, n: (prev(b, n), z_off + j)),
                  pl.BlockSpec((tc, LANES), lambda b, j, n: (cur(b, n), 0)),
                  pl.BlockSpec((1, 1, hb, tc), lambda b, j, n: (b, j, 0, jnp.minimum(n, nt - 1))),
                  pl.BlockSpec((1, LANES), lambda b, j, n: (0, 0)),
                  pl.BlockSpec((1, LANES), lambda b, j, n: (0, 0)),
                  pl.BlockSpec((1, hb, 1), lambda b, j, n: (j, 0, 0)),
                  pl.BlockSpec((1, hb, 1), lambda b, j, n: (j, 0, 0)),
                  pl.BlockSpec((1, hd), lambda b, j, n: (0, 0))],
        out_specs=pl.BlockSpec((tc, hb * hd), lambda b, j, n: (prev(b, n), j)),
        out_shape=jax.ShapeDtypeStruct((t, v_dim), BF16),
        scratch_shapes=[pltpu.VMEM((hd, hb * hd), F32),
                        pltpu.VMEM((2, nc, hd + cs, hb * hd), BF16),
                        pltpu.VMEM((2, nc, hd, hb * hd), F32),
                        pltpu.VMEM((2, nc, cs, hb * hd), F32),
                        pltpu.VMEM((2, nc, 1, hb * hd), F32)],
        compiler_params=_params(("parallel", "parallel", "arbitrary")),
        name="gdn_delta_rule",
    )(qkv, qkv, k_t, qkv, proj, ba, a_t, alr, dtr, alc, dtc, onorm_g.astype(F32).reshape(1, hd))


def _sb_kernel(q_ref, k_ref, v_ref, o_ref, acc_ref, lr_ref, zn_ref, wt_ref, *, tk):
    i = pl.program_id(2)
    tq = q_ref.shape[1]
    ks = 2 * tk
    nd = tq // ks

    acc_ref[...] = jnp.zeros_like(acc_ref)
    lr_ref[...] = jnp.zeros_like(lr_ref)

    ri = lax.broadcasted_iota(jnp.int32, (2 * tk, tk + LANES), 0) & (tk - 1)
    ci = lax.broadcasted_iota(jnp.int32, (2 * tk, tk + LANES), 1)
    ucat = jnp.where((ci >= tk) | (ri >= ci), 1.0, 0.0).astype(BF16)

    def scores(q, start):
        return _dot_nt(q, k_ref[0, pl.ds(start, ks), :])

    def log_parts(zn, mask):
        nabs = lax.bitcast_convert_type(
            lax.bitcast_convert_type(zn, jnp.uint32) | jnp.uint32(0x80000000), F32)
        l1 = jnp.minimum(zn, 0.0) - jnp.log2(1.0 + jnp.exp2(nabs))
        if mask is not None:
            l1 = jnp.where(mask, l1, 0.0)
        hi_f = lax.bitcast_convert_type(
            lax.bitcast_convert_type(l1, jnp.uint32) & jnp.uint32(0xFFFF0000), F32)
        hi = hi_f.astype(BF16)
        lo = (l1 - hi_f).astype(BF16)
        return (jnp.concatenate([hi[:, :tk], lo[:, :tk]], axis=1),
                jnp.concatenate([hi[:, tk:], lo[:, tk:]], axis=1))

    def suffix_sums(parts):
        return _dot(parts[0], ucat), _dot(parts[1], ucat)

    def exp_weights(zns, sums, lr, mask):
        wts = []
        for zn, (sa, sb) in zip(zns, sums):
            lra = lr + sb[:, tk:]
            wb = jnp.exp2(sb[:, :tk] + lr - zn[:, tk:])
            wa = jnp.exp2(sa[:, :tk] + lra - zn[:, :tk])
            w = jnp.concatenate([wa, wb], axis=1)
            if mask is not None:
                w = jnp.where(mask, w, 0.0)
            wts.append(w.astype(BF16))
            lr = lra + sa[:, tk:]
        return wts, lr

    def weights(zns, lr, mask):
        sums = [suffix_sums(log_parts(zn, mask)) for zn in zns]
        return exp_weights(zns, sums, lr, mask)

    base = pl.multiple_of(i * tq, tq)
    for dt in range(nd - 1, -1, -1):
        r0 = dt * ks
        rows = tq - r0
        start = pl.multiple_of(base + r0, ks)
        qi = lax.broadcasted_iota(jnp.int32, (rows, ks), 0)
        ki = lax.broadcasted_iota(jnp.int32, (rows, ks), 1)
        zn = scores(q_ref[0, r0:tq, :], start)
        wts, lr = weights([zn], lr_ref[r0:tq, :], ki < qi)
        acc_ref[r0:tq, :] += _dot(wts[0], v_ref[0, pl.ds(start, ks), :])
        lr_ref[r0:tq, :] = lr

    def window(t, u):
        tile = jnp.maximum(i - 1 - t, 0)
        return pl.multiple_of(tile * tq + (nd - 1 - u) * ks, ks)

    q = q_ref[0]
    for u in range(nd):
        zn_ref[u] = scores(q, window(0, u))
    wt_ref[...] = jnp.zeros_like(wt_ref)

    def body(t, carry):
        tprev = jnp.maximum(t - 1, 0)
        nh = 2
        hr = tq // nh
        units = [(h, u) for u in range(nd) for h in range(nh)]
        lrs = [lr_ref[h * hr:(h + 1) * hr, :] for h in range(nh)]
        pvs = [None] * nh

        def zn_of(h, u):
            return zn_ref[u, h * hr:(h + 1) * hr, :]

        zcur = zn_of(*units[0])
        pcur = log_parts(zcur, None)
        for k, (h, u) in enumerate(units):
            rs = slice(h * hr, (h + 1) * hr)
            if k + 1 < len(units):
                znext = zn_of(*units[k + 1])
                pnext = log_parts(znext, None)
            term = _dot(wt_ref[u, rs, :], v_ref[0, pl.ds(window(tprev, u), ks), :])
            pvs[h] = term if pvs[h] is None else pvs[h] + term
            sums = suffix_sums(pcur)
            nxt = scores(q_ref[0, rs, :], window(t + 1, u))
            wts, lrs[h] = exp_weights([zcur], [sums], lrs[h], None)
            wt_ref[u, rs, :] = wts[0]
            zn_ref[u, rs, :] = nxt
            if k + 1 < len(units):
                zcur, pcur = znext, pnext
        for h in range(nh):
            acc_ref[h * hr:(h + 1) * hr, :] += pvs[h]
            lr_ref[h * hr:(h + 1) * hr, :] = lrs[h]
        return carry

    lax.fori_loop(0, i, body, 0)
    pv = None
    for u in range(nd):
        term = _dot(wt_ref[u], v_ref[0, pl.ds(window(jnp.maximum(i - 1, 0), u), ks), :])
        pv = term if pv is None else pv + term
    o_ref[0] = (acc_ref[...] + pv).astype(o_ref.dtype)


def _sb_attention(q, kv, head_dim):
    b, s, d = q.shape
    assert head_dim == LANES
    nh = d // head_dim
    tk = LANES
    tq = _pick(s, (512, 256))
    assert tq % (2 * tk) == 0
    return pl.pallas_call(
        functools.partial(_sb_kernel, tk=tk),
        grid=(b, nh, s // tq),
        in_specs=[pl.BlockSpec((1, tq, head_dim), lambda b_, h, i: (b_, i, h)),
                  pl.BlockSpec((1, s, head_dim), lambda b_, h, i: (b_, 0, h)),
                  pl.BlockSpec((1, s, head_dim), lambda b_, h, i: (b_, 0, nh + h))],
        out_specs=pl.BlockSpec((1, tq, head_dim), lambda b_, h, i: (b_, i, h)),
        out_shape=jax.ShapeDtypeStruct((b, s, d), BF16),
        scratch_shapes=[pltpu.VMEM((tq, head_dim), F32), pltpu.VMEM((tq, LANES), F32),
                        pltpu.VMEM((tq // (2 * tk), tq, 2 * tk), F32),
                        pltpu.VMEM((tq // (2 * tk), tq, 2 * tk), BF16)],
        compiler_params=_params(("parallel", "parallel", "arbitrary")),
        name="sb_attention",
    )(q, kv, kv)


def kernel(x, c, ada_w, ada_b, norm_g, ffn_w_gu, ffn_w_down, gdn_w_in, gdn_conv_w, gdn_a_log,
           gdn_dt_bias, gdn_onorm_g, gdn_w_out, kv_ada_w, kv_ada_b, kv_norm_g, kv_w, k_norm_g,
           sb_w_q, sb_q_norm_g, sb_w_out):
    batch, seq, d = x.shape
    depth = ada_w.shape[0]
    n_a = gdn_w_in.shape[0]
    n_heads = gdn_a_log.shape[1]
    hd = gdn_onorm_g.shape[1]
    v_dim = n_heads * hd
    conv_ch = gdn_conv_w.shape[2]
    qk_dim = (conv_ch - v_dim) // 2
    sb_hd = k_norm_g.shape[0]
    t = batch * seq
    assert batch <= SUBLANES and 2 * n_heads <= LANES

    c8 = jnp.pad(c.astype(F32), ((0, SUBLANES - batch), (0, 0)))
    mod_all = _ada(c8, ada_w, ada_b)[:, :batch].reshape(depth, batch, 3, 3, d)
    kv_mod = _ada(c8, kv_ada_w[None], kv_ada_b[None])[0, :batch].reshape(batch, 2, d)
    kv_mod = jnp.concatenate([kv_mod, jnp.zeros((batch, 1, d), F32)], axis=1)

    xf = x.reshape(t, d)
    kv = None
    for l in range(depth):
        xf = _ffn(xf, mod_all[l, :, 0], norm_g[l, 0], ffn_w_gu[l, 0].astype(BF16),
                  ffn_w_down[l, 0].astype(BF16), seq)
        mod = mod_all[l, :, 1]
        if l < n_a:
            w_in = gdn_w_in[l]
            n_main = conv_ch + v_dim
            w_ba = jnp.pad(w_in[:, n_main:], ((0, 0), (0, LANES - 2 * n_heads))).astype(BF16)
            proj, ba = _proj(xf, mod, norm_g[l, 1], w_in[:, :n_main].astype(BF16), seq, F32,
                             w_extra=w_ba)
            qkv = _gdn_conv(proj, gdn_conv_w[l], seq, qk_dim, hd)
            a_t = jnp.transpose(ba.reshape(batch, seq, LANES)[:, :, n_heads:2 * n_heads], (0, 2, 1))
            k_t = jnp.transpose(qkv[:, qk_dim:2 * qk_dim])
            og = _gdn_core(qkv, k_t, proj, ba, a_t, gdn_a_log[l], gdn_dt_bias[l], gdn_onorm_g[l],
                           batch, seq, qk_dim, v_dim, n_heads, hd, conv_ch)
            xf = _out_proj(og, gdn_w_out[l].astype(BF16), xf, mod, seq)
        else:
            jl = l - n_a
            q = _proj(xf, mod, norm_g[l, 1], sb_w_q[jl].astype(BF16), seq, BF16,
                      head_gain=sb_q_norm_g[jl], out_scale=-LOG2E * float(sb_hd) ** -0.5)
            o = _sb_attention(q.reshape(batch, seq, d), kv, sb_hd)
            xf = _out_proj(o.reshape(t, d), sb_w_out[jl].astype(BF16), xf, mod, seq)
        xf = _ffn(xf, mod_all[l, :, 2], norm_g[l, 2], ffn_w_gu[l, 1].astype(BF16),
                  ffn_w_down[l, 1].astype(BF16), seq)
        if l == n_a - 1:
            kv = _proj(xf, kv_mod, kv_norm_g, kv_w.astype(BF16), seq, BF16, head_gain=k_norm_g,
                       norm_cols=d).reshape(batch, seq, 2 * d)
    return xf.reshape(batch, seq, d)
```

```python
import functools

import jax
import jax.numpy as jnp
from jax import lax
from jax.experimental import pallas as pl
from jax.experimental.pallas import tpu as pltpu

F32 = jnp.float32
BF16 = jnp.bfloat16
EPS = 1e-6
LANES = 128
SUBLANES = 8
GDN_CHUNK = 64
VMEM_LIMIT = 56 * 1024 * 1024
LOG2E = 1.4426950408889634


def _pick(n, candidates):
    for c in candidates:
        if c <= n and n % c == 0:
            return c
    return n


def _params(sem):
    return pltpu.CompilerParams(dimension_semantics=sem, vmem_limit_bytes=VMEM_LIMIT)


def _silu(x):
    return x * jax.nn.sigmoid(x)


def _softplus(x):
    return jnp.maximum(x, 0.0) + jnp.log1p(jnp.exp(-jnp.abs(x)))


def _dot(a, b):
    return jnp.dot(a, b, preferred_element_type=F32)


def _dot_nt(a, b):
    return lax.dot_general(a, b, (((1,), (1,)), ((), ())), preferred_element_type=F32)


def _norm_mod(x, g, shift, scale):
    ms = jnp.mean(x * x, axis=-1, keepdims=True)
    y = x * lax.rsqrt(ms + EPS) * g
    return y * (1.0 + scale) + shift


def _ada_kernel(c_ref, w_ref, b_ref, o_ref):
    ca = _silu(c_ref[...])
    o_ref[0] = _dot(ca, w_ref[0]) + b_ref[0]


def _ada(c8, w, b):
    nl, d, n = w.shape
    tn = _pick(n, (1024, 512, 256, 128))
    return pl.pallas_call(
        _ada_kernel,
        grid=(nl, n // tn),
        in_specs=[pl.BlockSpec((SUBLANES, d), lambda l, j: (0, 0)),
                  pl.BlockSpec((1, d, tn), lambda l, j: (l, 0, j)),
                  pl.BlockSpec((1, 1, tn), lambda l, j: (l, 0, j))],
        out_specs=pl.BlockSpec((1, SUBLANES, tn), lambda l, j: (l, 0, j)),
        out_shape=jax.ShapeDtypeStruct((nl, SUBLANES, n), F32),
        compiler_params=_params(("parallel", "parallel")),
        name="ada_proj",
    )(c8, w, b.reshape(nl, 1, n))


def _ffn_kernel(x_ref, mod_ref, g_ref, wg_ref, wu_ref, wd_ref, o_ref, h_ref, acc_ref):
    j = pl.program_id(1)

    @pl.when(j == 0)
    def _():
        h = _norm_mod(x_ref[...], g_ref[...], mod_ref[0, 0:1, :], mod_ref[0, 1:2, :])
        h_ref[...] = h.astype(BF16)
        acc_ref[...] = jnp.zeros_like(acc_ref)

    h = h_ref[...]
    gate = _dot(h, wg_ref[...])
    up = _dot(h, wu_ref[...])
    a = (_silu(gate) * up).astype(BF16)
    acc_ref[...] += _dot(a, wd_ref[...])

    @pl.when(j == pl.num_programs(1) - 1)
    def _():
        o_ref[...] = x_ref[...] + 0.5 * (1.0 + mod_ref[0, 2:3, :]) * acc_ref[...]


def _ffn(x, mod, g, w_gu, w_down, seq):
    t, d = x.shape
    f = w_down.shape[0]
    tm = _pick(seq, (512, 256, 128, 64, 32, 16, 8))
    tf = _pick(f, (512, 256, 128))
    tpb = seq // tm
    nf = f // tf
    return pl.pallas_call(
        _ffn_kernel,
        grid=(t // tm, nf),
        in_specs=[pl.BlockSpec((tm, d), lambda i, j: (i, 0)),
                  pl.BlockSpec((1, 3, d), lambda i, j: (i // tpb, 0, 0)),
                  pl.BlockSpec((1, d), lambda i, j: (0, 0)),
                  pl.BlockSpec((d, tf), lambda i, j: (0, j)),
                  pl.BlockSpec((d, tf), lambda i, j: (0, j + nf)),
                  pl.BlockSpec((tf, d), lambda i, j: (j, 0))],
        out_specs=pl.BlockSpec((tm, d), lambda i, j: (i, 0)),
        out_shape=jax.ShapeDtypeStruct((t, d), F32),
        scratch_shapes=[pltpu.VMEM((tm, d), BF16), pltpu.VMEM((tm, d), F32)],
        compiler_params=_params(("parallel", "arbitrary")),
        name="ffn",
    )(x, mod, g.reshape(1, d), w_gu, w_gu, w_down)


def _proj_kernel(*refs, head_dim, out_scale, n_norm_tiles, has_extra):
    if has_extra:
        x_ref, mod_ref, g_ref, w_ref, hg_ref, wx_ref, o_ref, ox_ref, h_ref = refs
    else:
        x_ref, mod_ref, g_ref, w_ref, hg_ref, o_ref, h_ref = refs
    j = pl.program_id(1)

    @pl.when(j == 0)
    def _():
        h = _norm_mod(x_ref[...], g_ref[...], mod_ref[0, 0:1, :], mod_ref[0, 1:2, :])
        h_ref[...] = h.astype(BF16)
        if has_extra:
            ox_ref[...] = _dot(h_ref[...], wx_ref[...])

    y = _dot(h_ref[...], w_ref[...])

    def plain():
        o_ref[...] = y.astype(o_ref.dtype)

    def head_normed():
        for s in range(0, y.shape[1], head_dim):
            yh = y[:, s:s + head_dim]
            ms = jnp.mean(yh * yh, axis=-1, keepdims=True)
            yn = yh * lax.rsqrt(ms + EPS) * hg_ref[:, s:s + head_dim]
            o_ref[:, s:s + head_dim] = (yn * out_scale).astype(o_ref.dtype)

    if head_dim is None:
        plain()
    elif n_norm_tiles is None:
        head_normed()
    else:
        pl.when(j < n_norm_tiles)(head_normed)
        pl.when(j >= n_norm_tiles)(plain)


def _proj(x, mod, g, w, seq, out_dtype, head_gain=None, out_scale=1.0, norm_cols=None, w_extra=None):
    t, d = x.shape
    n = w.shape[1]
    tm = _pick(seq, (512, 256, 128, 64, 32, 16, 8))
    tn = _pick(n if norm_cols is None else norm_cols, (1024, 512, 256, 128))
    assert n % tn == 0
    tpb = seq // tm
    if head_gain is None:
        head_dim = None
        hg = jnp.zeros((1, tn), F32)
    else:
        head_dim = head_gain.shape[0]
        hg = jnp.tile(head_gain.astype(F32), tn // head_dim).reshape(1, tn)
    n_norm_tiles = None if norm_cols is None else norm_cols // tn
    has_extra = w_extra is not None
    in_specs = [pl.BlockSpec((tm, d), lambda i, j: (i, 0)),
                pl.BlockSpec((1, 3, d), lambda i, j: (i // tpb, 0, 0)),
                pl.BlockSpec((1, d), lambda i, j: (0, 0)),
                pl.BlockSpec((d, tn), lambda i, j: (0, j)),
                pl.BlockSpec((1, tn), lambda i, j: (0, 0))]
    out_specs = pl.BlockSpec((tm, tn), lambda i, j: (i, j))
    out_shape = jax.ShapeDtypeStruct((t, n), out_dtype)
    args = [x, mod, g.reshape(1, d), w, hg]
    if has_extra:
        in_specs.append(pl.BlockSpec((d, LANES), lambda i, j: (0, 0)))
        out_specs = [out_specs, pl.BlockSpec((tm, LANES), lambda i, j: (i, 0))]
        out_shape = [out_shape, jax.ShapeDtypeStruct((t, LANES), F32)]
        args.append(w_extra)
    return pl.pallas_call(
        functools.partial(_proj_kernel, head_dim=head_dim, out_scale=out_scale,
                          n_norm_tiles=n_norm_tiles, has_extra=has_extra),
        grid=(t // tm, n // tn),
        in_specs=in_specs,
        out_specs=out_specs,
        out_shape=out_shape,
        scratch_shapes=[pltpu.VMEM((tm, d), BF16)],
        compiler_params=_params(("parallel", "arbitrary")),
        name="norm_proj",
    )(*args)


def _out_proj_kernel(a_ref, w_ref, x_ref, mod_ref, o_ref):
    y = _dot(a_ref[...], w_ref[...])
    o_ref[...] = x_ref[...] + (1.0 + mod_ref[0, 2:3, :]) * y


def _out_proj(a, w, x, mod, seq):
    t, k = a.shape
    d = w.shape[1]
    tm = _pick(seq, (512, 256, 128, 64, 32, 16, 8))
    tn = _pick(d, (1024, 512, 256, 128))
    tpb = seq // tm
    return pl.pallas_call(
        _out_proj_kernel,
        grid=(t // tm, d // tn),
        in_specs=[pl.BlockSpec((tm, k), lambda i, j: (i, 0)),
                  pl.BlockSpec((k, tn), lambda i, j: (0, j)),
                  pl.BlockSpec((tm, tn), lambda i, j: (i, j)),
                  pl.BlockSpec((1, 3, tn), lambda i, j: (i // tpb, 0, j))],
        out_specs=pl.BlockSpec((tm, tn), lambda i, j: (i, j)),
        out_shape=jax.ShapeDtypeStruct((t, d), F32),
        compiler_params=_params(("parallel", "arbitrary")),
        name="out_proj",
    )(a, w, x, mod)


def _conv_kernel(prev_ref, x_ref, w_ref, o_ref, xx_ref, *, tiles_per_batch, n_q_tiles, n_qk_tiles,
                 head_dim, q_scale):
    i = pl.program_id(0)
    j = pl.program_id(1)
    tm = x_ref.shape[0]
    kw = w_ref.shape[0]
    first = (i % tiles_per_batch) == 0
    xx_ref[0:SUBLANES, :] = jnp.where(first, 0.0, prev_ref[...])
    xx_ref[SUBLANES:SUBLANES + tm, :] = x_ref[...]
    y = None
    for tap in range(kw):
        off = SUBLANES - (kw - 1) + tap
        term = xx_ref[off:off + tm, :] * w_ref[tap:tap + 1, :]
        y = term if y is None else y + term
    y = _silu(y)

    @pl.when(j >= n_qk_tiles)
    def _():
        o_ref[...] = y

    @pl.when(j < n_qk_tiles)
    def _():
        scale = jnp.where(j < n_q_tiles, q_scale, 1.0)
        for s in range(0, y.shape[1], head_dim):
            yh = y[:, s:s + head_dim]
            ss = jnp.sum(yh * yh, axis=-1, keepdims=True)
            o_ref[:, s:s + head_dim] = yh * (lax.rsqrt(ss + EPS) * scale)


def _gdn_conv(proj, conv_w, seq, qk_dim, head_dim):
    t = proj.shape[0]
    kw, c = conv_w.shape
    tm = _pick(seq, (512, 256, 128, 64, 32, 16, 8))
    tc = _pick(qk_dim, (512, 256, 128))
    tpb = seq // tm
    rb = tm // SUBLANES
    kern = functools.partial(_conv_kernel, tiles_per_batch=tpb, n_q_tiles=qk_dim // tc,
                             n_qk_tiles=2 * qk_dim // tc, head_dim=head_dim,
                             q_scale=float(head_dim) ** -0.5)
    return pl.pallas_call(
        kern,
        grid=(t // tm, c // tc),
        in_specs=[pl.BlockSpec((SUBLANES, tc), lambda i, j: (jnp.maximum(i * rb - 1, 0), j)),
                  pl.BlockSpec((tm, tc), lambda i, j: (i, j)),
                  pl.BlockSpec((kw, tc), lambda i, j: (0, j))],
        out_specs=pl.BlockSpec((tm, tc), lambda i, j: (i, j)),
        out_shape=jax.ShapeDtypeStruct((t, c), F32),
        scratch_shapes=[pltpu.VMEM((tm + SUBLANES, tc), F32)],
        compiler_params=_params(("parallel", "parallel")),
        name="gdn_conv",
    )(proj, proj, conv_w)


def _split_hl(a):
    hi = a.astype(BF16)
    lo = (a - hi.astype(F32)).astype(BF16)
    return hi, lo


def _split3(a):
    h1 = a.astype(BF16)
    r1 = a - h1.astype(F32)
    h2 = r1.astype(BF16)
    h3 = (r1 - h2.astype(F32)).astype(BF16)
    return h1, h2, h3


def _gdn_kernel(q_ref, k_ref, kt_ref, v_ref, z_ref, ba_ref, at_ref, alr_ref, dtr_ref, alc_ref,
                dtc_ref, on_ref, o_ref, s_ref, cq_ref, bm_ref, om_ref, gl_ref, *, n_heads):
    jb = pl.program_id(1)
    n = pl.program_id(2)
    tc = q_ref.shape[0]
    hd = q_ref.shape[1]
    cs = GDN_CHUNK
    nc = tc // cs
    wslot = n % 2
    rslot = 1 - wslot

    @pl.when(n == 0)
    def _():
        s_ref[...] = jnp.zeros_like(s_ref)
        cq_ref[1] = jnp.zeros_like(cq_ref[1])
        bm_ref[1] = jnp.zeros_like(bm_ref[1])
        om_ref[1] = jnp.zeros_like(om_ref[1])
        gl_ref[1] = jnp.zeros_like(gl_ref[1])

    lane = lax.broadcasted_iota(jnp.int32, (cs, 2 * cs), 1)
    left = lane < cs
    lane2 = lax.broadcasted_iota(jnp.int32, (cs, 2 * hd), 1)
    left2 = lane2 < hd
    left_s = lax.broadcasted_iota(jnp.int32, (1, 2 * hd), 1) < hd
    ii = lax.broadcasted_iota(jnp.int32, (cs, 2 * cs), 0)
    jj = lane & (cs - 1)
    incl = ii >= jj
    strict = ii > jj
    eye = jnp.where(ii == jj, 1.0, 0.0).astype(F32)
    zb = jnp.zeros((cs, hd), BF16)
    on_g = on_ref[...]
    h0 = jb * 2
    si = lax.broadcasted_iota(jnp.int32, (cs, 3 * cs), 0)
    sj = lax.broadcasted_iota(jnp.int32, (cs, 3 * cs), 1) % cs
    low3 = jnp.where(si >= sj, 1.0, 0.0).astype(BF16)
    ui = lax.broadcasted_iota(jnp.int32, (3 * cs, cs), 0) % cs
    uj = lax.broadcasted_iota(jnp.int32, (3 * cs, cs), 1)
    up3 = jnp.where(ui <= uj, 1.0, 0.0).astype(BF16)

    def col(x, idx):
        return jnp.sum(jnp.where(lane == idx, x, 0.0), axis=1, keepdims=True)

    def bd(x):
        zero = jnp.zeros_like(x)
        return jnp.concatenate([jnp.where(left, x, zero), jnp.where(left, zero, x)], axis=0)

    def bd_wide(x):
        zero = jnp.zeros((x.shape[0], hd), x.dtype)
        return jnp.concatenate([jnp.concatenate([x[:, :hd], zero], axis=1),
                                jnp.concatenate([zero, x[:, hd:]], axis=1)], axis=0)

    def recurrence(c):
        st = s_ref[...]
        r = _dot(cq_ref[rslot, c], bd_wide(st.astype(BF16)))
        s_ref[...] = st * gl_ref[rslot, c] + bm_ref[rslot, c] - r[:hd]
        o = r[hd:] + om_ref[rslot, c]
        r0 = c * cs
        for hh in range(2):
            oh = o[:, hh * hd:(hh + 1) * hd]
            ms = jnp.mean(oh * oh, axis=-1, keepdims=True)
            zc = z_ref[r0:r0 + cs, hh * hd:(hh + 1) * hd]
            out = oh * lax.rsqrt(ms + EPS) * on_g * _silu(zc)
            o_ref[r0:r0 + cs, hh * hd:(hh + 1) * hd] = out.astype(o_ref.dtype)

    pending = list(range(nc))

    def run_recurrence(k):
        for _ in range(k):
            if pending:
                recurrence(pending.pop(0))

    ba = ba_ref[...]
    beta_all = jax.nn.sigmoid(ba)
    g_all = -jnp.exp(alr_ref[...]) * _softplus(ba + dtr_ref[...])
    g_row = -jnp.exp(alc_ref[0]) * _softplus(at_ref[0, 0] + dtc_ref[0])
    gcs, grs = [], []
    for c in range(nc):
        r0 = c * cs
        gcs.append(_dot(low3, jnp.concatenate(_split3(g_all[r0:r0 + cs]), axis=0)))
        grs.append(_dot(jnp.concatenate(_split3(g_row[:, r0:r0 + cs]), axis=1), up3))
    run_recurrence(max(1, nc // 4))

    chunks = []
    for c in range(nc):
        r0 = c * cs
        b0 = col(beta_all[r0:r0 + cs], h0)
        b1 = col(beta_all[r0:r0 + cs], h0 + 1)
        g0 = col(gcs[c], n_heads + h0)
        g1 = col(gcs[c], n_heads + h0 + 1)
        grow = jnp.concatenate([grs[c][0:1], grs[c][1:2]], axis=1)
        glast = jnp.where(left[0:1], grs[c][0:1, cs - 1:cs], grs[c][1:2, cs - 1:cs])
        gcol = jnp.where(left, g0, g1)
        bcol = jnp.where(left, b0, b1)
        dec = jnp.where(incl, jnp.exp(jnp.where(incl, gcol - grow, 0.0)), 0.0)
        qc = q_ref[r0:r0 + cs, :]
        kc = k_ref[r0:r0 + cs, :]
        kcb = kc.astype(BF16)
        qkk = _dot_nt(jnp.concatenate([qc.astype(BF16), kcb], axis=0),
                      jnp.concatenate([kcb, kcb], axis=0))
        attn = qkk[:cs] * dec
        m = bcol * qkk[cs:] * jnp.where(strict, dec, 0.0)
        b2 = jnp.where(left2, b0, b1)
        eg2 = jnp.where(left2, jnp.exp(g0), jnp.exp(g1))
        vb = (v_ref[r0:r0 + cs, :] * b2).astype(BF16)
        kbe = (jnp.concatenate([kc, kc], axis=1) * (b2 * eg2)).astype(BF16)
        rmat = jnp.concatenate(
            [jnp.concatenate([vb[:, :hd], zb, kbe[:, :hd], zb], axis=1),
             jnp.concatenate([zb, vb[:, hd:], zb, kbe[:, hd:]], axis=1)], axis=0)
        qg = jnp.concatenate([qc, qc], axis=1) * eg2
        ktc = kt_ref[:, r0:r0 + cs]
        kdt = jnp.concatenate([ktc, ktc], axis=1) * jnp.exp(glast - grow)
        gl_ref[wslot, c] = jnp.where(left_s, jnp.exp(glast[:, 0:1]), jnp.exp(glast[:, cs:cs + 1]))
        chunks.append(dict(m=m, ak=jnp.concatenate([attn, kdt], axis=0).astype(BF16),
                           rmat=rmat, qg=qg))

    ts = [eye - ch["m"] for ch in chunks]
    ps = [_split_hl(ch["m"]) for ch in chunks]
    k = 2
    while k < cs:
        p2s = [_dot(jnp.concatenate([ph, plo, ph], axis=1),
                    jnp.concatenate([bd(ph), bd(ph), bd(plo)], axis=0)) for ph, plo in ps]
        ps = [_split_hl(p2) for p2 in p2s]
        if k in (4, 16):
            run_recurrence(max(1, nc // 4))
        new_ts = []
        for (p2h, p2l), t in zip(ps, ts):
            th, tl = _split_hl(t)
            rhs2 = jnp.concatenate([bd(p2h), bd(p2h), bd(p2l)], axis=0)
            new_ts.append(t + _dot(jnp.concatenate([th, tl, th], axis=1), rhs2))
        ts = new_ts
        k *= 2
    run_recurrence(max(1, nc // 4))

    uws = []
    for ch, t in zip(chunks, ts):
        th, tl = _split_hl(t)
        uws.append(_dot(jnp.concatenate([th, tl], axis=1),
                        jnp.concatenate([ch["rmat"], ch["rmat"]], axis=0)))
    run_recurrence(nc)

    for c, (ch, uw) in enumerate(zip(chunks, uws)):
        uwb = uw.astype(BF16)
        x = _dot(ch["ak"], jnp.concatenate([bd_wide(uwb[:, :2 * hd]), bd_wide(uwb[:, 2 * hd:])],
                                           axis=1))
        om_ref[wslot, c] = x[:cs, :2 * hd]
        bm_ref[wslot, c] = x[cs:, :2 * hd]
        cq_ref[wslot, c] = jnp.concatenate([x[cs:, 2 * hd:], ch["qg"] - x[:cs, 2 * hd:]],
                                           axis=0).astype(BF16)


def _gdn_core(qkv, k_t, proj, ba, a_t, a_log, dt_bias, onorm_g, batch, seq, qk_dim, v_dim, n_heads,
              head_dim, z_col0):
    t = qkv.shape[0]
    hd = head_dim
    qk_heads = qk_dim // hd
    hb = n_heads // qk_heads
    cs = GDN_CHUNK
    assert hb == 2 and 2 * cs == LANES and hd == LANES
    nblk = n_heads // hb
    tc = _pick(seq, (512, 256, 128, 64))
    nt = seq // tc
    nc = tc // cs
    pad = jnp.zeros((LANES - 2 * n_heads,), F32)
    zeros_h = jnp.zeros((n_heads,), F32)
    alr = jnp.concatenate([zeros_h, a_log.astype(F32), pad]).reshape(1, LANES)
    dtr = jnp.concatenate([zeros_h, dt_bias.astype(F32), pad]).reshape(1, LANES)
    alc = a_log.astype(F32).reshape(nblk, hb, 1)
    dtc = dt_bias.astype(F32).reshape(nblk, hb, 1)
    a_t = a_t.reshape(batch, nblk, hb, seq)
    kern = functools.partial(_gdn_kernel, n_heads=n_heads)
    k_off = qk_dim // hd
    v_off = 2 * qk_dim // (hb * hd)
    z_off = z_col0 // (hb * hd)

    def cur(b, n):
        return b * nt + jnp.minimum(n, nt - 1)

    def prev(b, n):
        return b * nt + jnp.maximum(n - 1, 0)

    return pl.pallas_call(
        kern,
        grid=(batch, nblk, nt + 1),
        in_specs=[pl.BlockSpec((tc, hd), lambda b, j, n: (cur(b, n), j)),
                  pl.BlockSpec((tc, hd), lambda b, j, n: (cur(b, n), k_off + j)),
                  pl.BlockSpec((hd, tc), lambda b, j, n: (j, cur(b, n))),
                  pl.BlockSpec((tc, hb * hd), lambda b, j, n: (cur(b, n), v_off + j)),
                  pl.BlockSpec((tc, hb * hd), lambda b, j, n: (prev(b, n), z_off + j)),
                  pl.BlockSpec((tc, LANES), lambda b, j, n: (cur(b, n), 0)),
                  pl.BlockSpec((1, 1, hb, tc), lambda b, j, n: (b, j, 0, jnp.minimum(n, nt - 1))),
                  pl.BlockSpec((1, LANES), lambda b, j, n: (0, 0)),
                  pl.BlockSpec((1, LANES), lambda b, j, n: (0, 0)),
                  pl.BlockSpec((1, hb, 1), lambda b, j, n: (j, 0, 0)),
                  pl.BlockSpec((1, hb, 1), lambda b, j, n: (j, 0, 0)),
                  pl.BlockSpec((1, hd), lambda b, j, n: (0, 0))],
        out_specs=pl.BlockSpec((tc, hb * hd), lambda b, j, n: (prev(b, n), j)),
        out_shape=jax.ShapeDtypeStruct((t, v_dim), BF16),
        scratch_shapes=[pltpu.VMEM((hd, hb * hd), F32),
                        pltpu.VMEM((2, nc, hd + cs, hb * hd), BF16),
                        pltpu.VMEM((2, nc, hd, hb * hd), F32),
                        pltpu.VMEM((2, nc, cs, hb * hd), F32),
                        pltpu.VMEM((2, nc, 1, hb * hd), F32)],
        compiler_params=_params(("parallel", "parallel", "arbitrary")),
        name="gdn_delta_rule",
    )(qkv, qkv, k_t, qkv, proj, ba, a_t, alr, dtr, alc, dtc, onorm_g.astype(F32).reshape(1, hd))


def _sb_kernel(q_ref, k_ref, v_ref, o_ref, acc_ref, lr_ref, zn_ref, wt_ref, *, tk):
    i = pl.program_id(2)
    tq = q_ref.shape[1]
    ks = 2 * tk
    nd = tq // ks

    acc_ref[...] = jnp.zeros_like(acc_ref)
    lr_ref[...] = jnp.zeros_like(lr_ref)

    ri = lax.broadcasted_iota(jnp.int32, (2 * tk, tk + LANES), 0) & (tk - 1)
    ci = lax.broadcasted_iota(jnp.int32, (2 * tk, tk + LANES), 1)
    ucat = jnp.where((ci >= tk) | (ri >= ci), 1.0, 0.0).astype(BF16)

    def scores(q, start):
        return _dot_nt(q, k_ref[0, pl.ds(start, ks), :])

    def log_parts(zn, mask):
        nabs = lax.bitcast_convert_type(
            lax.bitcast_convert_type(zn, jnp.uint32) | jnp.uint32(0x80000000), F32)
        l1 = jnp.minimum(zn, 0.0) - jnp.log2(1.0 + jnp.exp2(nabs))
        if mask is not None:
            l1 = jnp.where(mask, l1, 0.0)
        hi_f = lax.bitcast_convert_type(
            lax.bitcast_convert_type(l1, jnp.uint32) & jnp.uint32(0xFFFF0000), F32)
        hi = hi_f.astype(BF16)
        lo = (l1 - hi_f).astype(BF16)
        return (jnp.concatenate([hi[:, :tk], lo[:, :tk]], axis=1),
                jnp.concatenate([hi[:, tk:], lo[:, tk:]], axis=1))

    def suffix_sums(parts):
        return _dot(parts[0], ucat), _dot(parts[1], ucat)

    def exp_weights(zns, sums, lr, mask):
        wts = []
        for zn, (sa, sb) in zip(zns, sums):
            lra = lr + sb[:, tk:]
            wb = jnp.exp2(sb[:, :tk] + lr - zn[:, tk:])
            wa = jnp.exp2(sa[:, :tk] + lra - zn[:, :tk])
            w = jnp.concatenate([wa, wb], axis=1)
            if mask is not None:
                w = jnp.where(mask, w, 0.0)
            wts.append(w.astype(BF16))
            lr = lra + sa[:, tk:]
        return wts, lr

    base = pl.multiple_of(i * tq, tq)
    diag = []
    for dt in range(nd - 1, -1, -1):
        r0 = dt * ks
        start = pl.multiple_of(base + r0, ks)
        qi = lax.broadcasted_iota(jnp.int32, (tq - r0, ks), 0)
        ki = lax.broadcasted_iota(jnp.int32, (tq - r0, ks), 1)
        diag.append(dict(r0=r0, start=start, mask=ki < qi, zn=scores(q_ref[0, r0:tq, :], start)))
    for dg in diag:
        dg["parts"] = log_parts(dg["zn"], dg["mask"])
    for dg in diag:
        dg["sums"] = suffix_sums(dg["parts"])
    for dg in diag:
        r0 = dg["r0"]
        wts, lr = exp_weights([dg["zn"]], [dg["sums"]], lr_ref[r0:tq, :], dg["mask"])
        acc_ref[r0:tq, :] += _dot(wts[0], v_ref[0, pl.ds(dg["start"], ks), :])
        lr_ref[r0:tq, :] = lr

    def window(t, u):
        tile = jnp.maximum(i - 1 - t, 0)
        return pl.multiple_of(tile * tq + (nd - 1 - u) * ks, ks)

    q = q_ref[0]
    for u in range(nd):
        zn_ref[u] = scores(q, window(0, u))
    wt_ref[...] = jnp.zeros_like(wt_ref)

    def body(t, carry):
        tprev = jnp.maximum(t - 1, 0)
        nh = 2
        hr = tq // nh
        units = [(h, u) for u in range(nd) for h in range(nh)]
        lrs = [lr_ref[h * hr:(h + 1) * hr, :] for h in range(nh)]
        pvs = [None] * nh

        def zn_of(h, u):
            return zn_ref[u, h * hr:(h + 1) * hr, :]

        zcur = zn_of(*units[0])
        pcur = log_parts(zcur, None)
        for k, (h, u) in enumerate(units):
            rs = slice(h * hr, (h + 1) * hr)
            if k + 1 < len(units):
                znext = zn_of(*units[k + 1])
                pnext = log_parts(znext, None)
            term = _dot(wt_ref[u, rs, :], v_ref[0, pl.ds(window(tprev, u), ks), :])
            pvs[h] = term if pvs[h] is None else pvs[h] + term
            sums = suffix_sums(pcur)
            nxt = scores(q_ref[0, rs, :], window(t + 1, u))
            wts, lrs[h] = exp_weights([zcur], [sums], lrs[h], None)
            wt_ref[u, rs, :] = wts[0]
            zn_ref[u, rs, :] = nxt
            if k + 1 < len(units):
                zcur, pcur = znext, pnext
        for h in range(nh):
            acc_ref[h * hr:(h + 1) * hr, :] += pvs[h]
            lr_ref[h * hr:(h + 1) * hr, :] = lrs[h]
        return carry

    lax.fori_loop(0, i, body, 0)
    pv = None
    for u in range(nd):
        term = _dot(wt_ref[u], v_ref[0, pl.ds(window(jnp.maximum(i - 1, 0), u), ks), :])
        pv = term if pv is None else pv + term
    o_ref[0] = (acc_ref[...] + pv).astype(o_ref.dtype)


def _sb_attention(q, kv, head_dim):
    b, s, d = q.shape
    assert head_dim == LANES
    nh = d // head_dim
    tk = LANES
    tq = _pick(s, (1024, 512, 256))
    assert tq % (2 * tk) == 0
    return pl.pallas_call(
        functools.partial(_sb_kernel, tk=tk),
        grid=(b, nh, s // tq),
        in_specs=[pl.BlockSpec((1, tq, head_dim), lambda b_, h, i: (b_, i, h)),
                  pl.BlockSpec((1, s, head_dim), lambda b_, h, i: (b_, 0, h)),
                  pl.BlockSpec((1, s, head_dim), lambda b_, h, i: (b_, 0, nh + h))],
        out_specs=pl.BlockSpec((1, tq, head_dim), lambda b_, h, i: (b_, i, h)),
        out_shape=jax.ShapeDtypeStruct((b, s, d), BF16),
        scratch_shapes=[pltpu.VMEM((tq, head_dim), F32), pltpu.VMEM((tq, LANES), F32),
                        pltpu.VMEM((tq // (2 * tk), tq, 2 * tk), F32),
                        pltpu.VMEM((tq // (2 * tk), tq, 2 * tk), BF16)],
        compiler_params=_params(("parallel", "parallel", "arbitrary")),
        name="sb_attention",
    )(q, kv, kv)


def kernel(x, c, ada_w, ada_b, norm_g, ffn_w_gu, ffn_w_down, gdn_w_in, gdn_conv_w, gdn_a_log,
           gdn_dt_bias, gdn_onorm_g, gdn_w_out, kv_ada_w, kv_ada_b, kv_norm_g, kv_w, k_norm_g,
           sb_w_q, sb_q_norm_g, sb_w_out):
    batch, seq, d = x.shape
    depth = ada_w.shape[0]
    n_a = gdn_w_in.shape[0]
    n_heads = gdn_a_log.shape[1]
    hd = gdn_onorm_g.shape[1]
    v_dim = n_heads * hd
    conv_ch = gdn_conv_w.shape[2]
    qk_dim = (conv_ch - v_dim) // 2
    sb_hd = k_norm_g.shape[0]
    t = batch * seq
    assert batch <= SUBLANES and 2 * n_heads <= LANES

    c8 = jnp.pad(c.astype(F32), ((0, SUBLANES - batch), (0, 0)))
    mod_all = _ada(c8, ada_w, ada_b)[:, :batch].reshape(depth, batch, 3, 3, d)
    kv_mod = _ada(c8, kv_ada_w[None], kv_ada_b[None])[0, :batch].reshape(batch, 2, d)
    kv_mod = jnp.concatenate([kv_mod, jnp.zeros((batch, 1, d), F32)], axis=1)

    xf = x.reshape(t, d)
    kv = None
    for l in range(depth):
        xf = _ffn(xf, mod_all[l, :, 0], norm_g[l, 0], ffn_w_gu[l, 0].astype(BF16),
                  ffn_w_down[l, 0].astype(BF16), seq)
        mod = mod_all[l, :, 1]
        if l < n_a:
            w_in = gdn_w_in[l]
            n_main = conv_ch + v_dim
            w_ba = jnp.pad(w_in[:, n_main:], ((0, 0), (0, LANES - 2 * n_heads))).astype(BF16)
            proj, ba = _proj(xf, mod, norm_g[l, 1], w_in[:, :n_main].astype(BF16), seq, F32,
                             w_extra=w_ba)
            qkv = _gdn_conv(proj, gdn_conv_w[l], seq, qk_dim, hd)
            a_t = jnp.transpose(ba.reshape(batch, seq, LANES)[:, :, n_heads:2 * n_heads], (0, 2, 1))
            k_t = jnp.transpose(qkv[:, qk_dim:2 * qk_dim])
            og = _gdn_core(qkv, k_t, proj, ba, a_t, gdn_a_log[l], gdn_dt_bias[l], gdn_onorm_g[l],
                           batch, seq, qk_dim, v_dim, n_heads, hd, conv_ch)
            xf = _out_proj(og, gdn_w_out[l].astype(BF16), xf, mod, seq)
        else:
            jl = l - n_a
            q = _proj(xf, mod, norm_g[l, 1], sb_w_q[jl].astype(BF16), seq, BF16,
                      head_gain=sb_q_norm_g[jl], out_scale=-LOG2E * float(sb_hd) ** -0.5)
            o = _sb_attention(q.reshape(batch, seq, d), kv, sb_hd)
            xf = _out_proj(o.reshape(t, d), sb_w_out[jl].astype(BF16), xf, mod, seq)
        xf = _ffn(xf, mod_all[l, :, 2], norm_g[l, 2], ffn_w_gu[l, 1].astype(BF16),
                  ffn_w_down[l, 1].astype(BF16), seq)
        if l == n_a - 1:
            kv = _proj(xf, kv_mod, kv_norm_g, kv_w.astype(BF16), seq, BF16, head_gain=k_norm_g,
                       norm_cols=d).reshape(batch, seq, 2 * d)
    return xf.reshape(batch, seq, d)
```

```python
import functools

import jax
import jax.numpy as jnp
from jax import lax
from jax.experimental import pallas as pl
from jax.experimental.pallas import tpu as pltpu

F32 = jnp.float32
BF16 = jnp.bfloat16
EPS = 1e-6
LANES = 128
SUBLANES = 8
GDN_CHUNK = 64
VMEM_LIMIT = 56 * 1024 * 1024
LOG2E = 1.4426950408889634


def _pick(n, candidates):
    for c in candidates:
        if c <= n and n % c == 0:
            return c
    return n


def _params(sem):
    return pltpu.CompilerParams(dimension_semantics=sem, vmem_limit_bytes=VMEM_LIMIT)


def _silu(x):
    return x * jax.nn.sigmoid(x)


def _softplus(x):
    return jnp.maximum(x, 0.0) + jnp.log1p(jnp.exp(-jnp.abs(x)))


def _dot(a, b):
    return jnp.dot(a, b, preferred_element_type=F32)


def _dot_nt(a, b):
    return lax.dot_general(a, b, (((1,), (1,)), ((), ())), preferred_element_type=F32)


def _norm_mod(x, g, shift, scale):
    ms = jnp.mean(x * x, axis=-1, keepdims=True)
    y = x * lax.rsqrt(ms + EPS) * g
    return y * (1.0 + scale) + shift


def _ada_kernel(c_ref, w_ref, b_ref, o_ref):
    ca = _silu(c_ref[...])
    o_ref[0] = _dot(ca, w_ref[0]) + b_ref[0]


def _ada(c8, w, b):
    nl, d, n = w.shape
    tn = _pick(n, (1024, 512, 256, 128))
    return pl.pallas_call(
        _ada_kernel,
        grid=(nl, n // tn),
        in_specs=[pl.BlockSpec((SUBLANES, d), lambda l, j: (0, 0)),
                  pl.BlockSpec((1, d, tn), lambda l, j: (l, 0, j)),
                  pl.BlockSpec((1, 1, tn), lambda l, j: (l, 0, j))],
        out_specs=pl.BlockSpec((1, SUBLANES, tn), lambda l, j: (l, 0, j)),
        out_shape=jax.ShapeDtypeStruct((nl, SUBLANES, n), F32),
        compiler_params=_params(("parallel", "parallel")),
        name="ada_proj",
    )(c8, w, b.reshape(nl, 1, n))


def _ffn_kernel(x_ref, mod_ref, g_ref, wg_ref, wu_ref, wd_ref, o_ref, h_ref, acc_ref):
    j = pl.program_id(1)

    @pl.when(j == 0)
    def _():
        h = _norm_mod(x_ref[...], g_ref[...], mod_ref[0, 0:1, :], mod_ref[0, 1:2, :])
        h_ref[...] = h.astype(BF16)
        acc_ref[...] = jnp.zeros_like(acc_ref)

    h = h_ref[...]
    gate = _dot(h, wg_ref[...])
    up = _dot(h, wu_ref[...])
    a = (_silu(gate) * up).astype(BF16)
    acc_ref[...] += _dot(a, wd_ref[...])

    @pl.when(j == pl.num_programs(1) - 1)
    def _():
        o_ref[...] = x_ref[...] + 0.5 * (1.0 + mod_ref[0, 2:3, :]) * acc_ref[...]


def _ffn(x, mod, g, w_gu, w_down, seq):
    t, d = x.shape
    f = w_down.shape[0]
    tm = _pick(seq, (512, 256, 128, 64, 32, 16, 8))
    tf = _pick(f, (512, 256, 128))
    tpb = seq // tm
    nf = f // tf
    return pl.pallas_call(
        _ffn_kernel,
        grid=(t // tm, nf),
        in_specs=[pl.BlockSpec((tm, d), lambda i, j: (i, 0)),
                  pl.BlockSpec((1, 3, d), lambda i, j: (i // tpb, 0, 0)),
                  pl.BlockSpec((1, d), lambda i, j: (0, 0)),
                  pl.BlockSpec((d, tf), lambda i, j: (0, j)),
                  pl.BlockSpec((d, tf), lambda i, j: (0, j + nf)),
                  pl.BlockSpec((tf, d), lambda i, j: (j, 0))],
        out_specs=pl.BlockSpec((tm, d), lambda i, j: (i, 0)),
        out_shape=jax.ShapeDtypeStruct((t, d), F32),
        scratch_shapes=[pltpu.VMEM((tm, d), BF16), pltpu.VMEM((tm, d), F32)],
        compiler_params=_params(("parallel", "arbitrary")),
        name="ffn",
    )(x, mod, g.reshape(1, d), w_gu, w_gu, w_down)


def _proj_kernel(*refs, head_dim, out_scale, n_norm_tiles, has_extra):
    if has_extra:
        x_ref, mod_ref, g_ref, w_ref, hg_ref, wx_ref, o_ref, ox_ref, h_ref = refs
    else:
        x_ref, mod_ref, g_ref, w_ref, hg_ref, o_ref, h_ref = refs
    j = pl.program_id(1)

    @pl.when(j == 0)
    def _():
        h = _norm_mod(x_ref[...], g_ref[...], mod_ref[0, 0:1, :], mod_ref[0, 1:2, :])
        h_ref[...] = h.astype(BF16)
        if has_extra:
            ox_ref[...] = _dot(h_ref[...], wx_ref[...])

    y = _dot(h_ref[...], w_ref[...])

    def plain():
        o_ref[...] = y.astype(o_ref.dtype)

    def head_normed():
        for s in range(0, y.shape[1], head_dim):
            yh = y[:, s:s + head_dim]
            ms = jnp.mean(yh * yh, axis=-1, keepdims=True)
            yn = yh * lax.rsqrt(ms + EPS) * hg_ref[:, s:s + head_dim]
            o_ref[:, s:s + head_dim] = (yn * out_scale).astype(o_ref.dtype)

    if head_dim is None:
        plain()
    elif n_norm_tiles is None:
        head_normed()
    else:
        pl.when(j < n_norm_tiles)(head_normed)
        pl.when(j >= n_norm_tiles)(plain)


def _proj(x, mod, g, w, seq, out_dtype, head_gain=None, out_scale=1.0, norm_cols=None, w_extra=None,
          n_cols=None):
    t, d = x.shape
    n = w.shape[1] if n_cols is None else n_cols
    tm = _pick(seq, (512, 256, 128, 64, 32, 16, 8))
    tn = _pick(n if norm_cols is None else norm_cols, (1024, 512, 256, 128))
    assert n % tn == 0
    tpb = seq // tm
    if head_gain is None:
        head_dim = None
        hg = jnp.zeros((1, tn), F32)
    else:
        head_dim = head_gain.shape[0]
        hg = jnp.tile(head_gain.astype(F32), tn // head_dim).reshape(1, tn)
    n_norm_tiles = None if norm_cols is None else norm_cols // tn
    has_extra = w_extra is not None
    in_specs = [pl.BlockSpec((tm, d), lambda i, j: (i, 0)),
                pl.BlockSpec((1, 3, d), lambda i, j: (i // tpb, 0, 0)),
                pl.BlockSpec((1, d), lambda i, j: (0, 0)),
                pl.BlockSpec((d, tn), lambda i, j: (0, j)),
                pl.BlockSpec((1, tn), lambda i, j: (0, 0))]
    out_specs = pl.BlockSpec((tm, tn), lambda i, j: (i, j))
    out_shape = jax.ShapeDtypeStruct((t, n), out_dtype)
    args = [x, mod, g.reshape(1, d), w, hg]
    if has_extra:
        in_specs.append(pl.BlockSpec((d, LANES), lambda i, j: (0, 0)))
        out_specs = [out_specs, pl.BlockSpec((tm, LANES), lambda i, j: (i, 0))]
        out_shape = [out_shape, jax.ShapeDtypeStruct((t, LANES), F32)]
        args.append(w_extra)
    return pl.pallas_call(
        functools.partial(_proj_kernel, head_dim=head_dim, out_scale=out_scale,
                          n_norm_tiles=n_norm_tiles, has_extra=has_extra),
        grid=(t // tm, n // tn),
        in_specs=in_specs,
        out_specs=out_specs,
        out_shape=out_shape,
        scratch_shapes=[pltpu.VMEM((tm, d), BF16)],
        compiler_params=_params(("parallel", "arbitrary")),
        name="norm_proj",
    )(*args)


def _out_proj_kernel(a_ref, w_ref, x_ref, mod_ref, o_ref):
    y = _dot(a_ref[...], w_ref[...])
    o_ref[...] = x_ref[...] + (1.0 + mod_ref[0, 2:3, :]) * y


def _out_proj(a, w, x, mod, seq):
    t, k = a.shape
    d = w.shape[1]
    tm = _pick(seq, (512, 256, 128, 64, 32, 16, 8))
    tn = _pick(d, (1024, 512, 256, 128))
    tpb = seq // tm
    return pl.pallas_call(
        _out_proj_kernel,
        grid=(t // tm, d // tn),
        in_specs=[pl.BlockSpec((tm, k), lambda i, j: (i, 0)),
                  pl.BlockSpec((k, tn), lambda i, j: (0, j)),
                  pl.BlockSpec((tm, tn), lambda i, j: (i, j)),
                  pl.BlockSpec((1, 3, tn), lambda i, j: (i // tpb, 0, j))],
        out_specs=pl.BlockSpec((tm, tn), lambda i, j: (i, j)),
        out_shape=jax.ShapeDtypeStruct((t, d), F32),
        compiler_params=_params(("parallel", "arbitrary")),
        name="out_proj",
    )(a, w, x, mod)


def _conv_kernel(prev_ref, x_ref, w_ref, o_ref, xx_ref, *, tiles_per_batch, n_q_tiles, n_qk_tiles,
                 head_dim, q_scale):
    i = pl.program_id(0)
    j = pl.program_id(1)
    tm = x_ref.shape[0]
    kw = w_ref.shape[0]
    first = (i % tiles_per_batch) == 0
    xx_ref[0:SUBLANES, :] = jnp.where(first, 0.0, prev_ref[...])
    xx_ref[SUBLANES:SUBLANES + tm, :] = x_ref[...]
    y = None
    for tap in range(kw):
        off = SUBLANES - (kw - 1) + tap
        term = xx_ref[off:off + tm, :] * w_ref[tap:tap + 1, :]
        y = term if y is None else y + term
    y = _silu(y)

    @pl.when(j >= n_qk_tiles)
    def _():
        o_ref[...] = y

    @pl.when(j < n_qk_tiles)
    def _():
        scale = jnp.where(j < n_q_tiles, q_scale, 1.0)
        for s in range(0, y.shape[1], head_dim):
            yh = y[:, s:s + head_dim]
            ss = jnp.sum(yh * yh, axis=-1, keepdims=True)
            o_ref[:, s:s + head_dim] = yh * (lax.rsqrt(ss + EPS) * scale)


def _gdn_conv(proj, conv_w, seq, qk_dim, head_dim):
    t = proj.shape[0]
    kw, c = conv_w.shape
    tm = _pick(seq, (512, 256, 128, 64, 32, 16, 8))
    tc = _pick(qk_dim, (512, 256, 128))
    tpb = seq // tm
    rb = tm // SUBLANES
    kern = functools.partial(_conv_kernel, tiles_per_batch=tpb, n_q_tiles=qk_dim // tc,
                             n_qk_tiles=2 * qk_dim // tc, head_dim=head_dim,
                             q_scale=float(head_dim) ** -0.5)
    return pl.pallas_call(
        kern,
        grid=(t // tm, c // tc),
        in_specs=[pl.BlockSpec((SUBLANES, tc), lambda i, j: (jnp.maximum(i * rb - 1, 0), j)),
                  pl.BlockSpec((tm, tc), lambda i, j: (i, j)),
                  pl.BlockSpec((kw, tc), lambda i, j: (0, j))],
        out_specs=pl.BlockSpec((tm, tc), lambda i, j: (i, j)),
        out_shape=jax.ShapeDtypeStruct((t, c), F32),
        scratch_shapes=[pltpu.VMEM((tm + SUBLANES, tc), F32)],
        compiler_params=_params(("parallel", "parallel")),
        name="gdn_conv",
    )(proj, proj, conv_w)


def _split_hl(a):
    hi = a.astype(BF16)
    lo = (a - hi.astype(F32)).astype(BF16)
    return hi, lo


def _split3(a):
    h1 = a.astype(BF16)
    r1 = a - h1.astype(F32)
    h2 = r1.astype(BF16)
    h3 = (r1 - h2.astype(F32)).astype(BF16)
    return h1, h2, h3


def _gdn_kernel(q_ref, k_ref, kt_ref, v_ref, z_ref, ba_ref, at_ref, alr_ref, dtr_ref, alc_ref,
                dtc_ref, on_ref, o_ref, s_ref, cq_ref, bm_ref, om_ref, gl_ref, *, n_heads):
    jb = pl.program_id(1)
    n = pl.program_id(2)
    tc = q_ref.shape[0]
    hd = q_ref.shape[1]
    cs = GDN_CHUNK
    nc = tc // cs
    wslot = n % 2
    rslot = 1 - wslot

    @pl.when(n == 0)
    def _():
        s_ref[...] = jnp.zeros_like(s_ref)
        cq_ref[1] = jnp.zeros_like(cq_ref[1])
        bm_ref[1] = jnp.zeros_like(bm_ref[1])
        om_ref[1] = jnp.zeros_like(om_ref[1])
        gl_ref[1] = jnp.zeros_like(gl_ref[1])

    lane = lax.broadcasted_iota(jnp.int32, (cs, 2 * cs), 1)
    left = lane < cs
    lane2 = lax.broadcasted_iota(jnp.int32, (cs, 2 * hd), 1)
    left2 = lane2 < hd
    left_s = lax.broadcasted_iota(jnp.int32, (1, 2 * hd), 1) < hd
    ii = lax.broadcasted_iota(jnp.int32, (cs, 2 * cs), 0)
    jj = lane & (cs - 1)
    incl = ii >= jj
    strict = ii > jj
    eye = jnp.where(ii == jj, 1.0, 0.0).astype(F32)
    zb = jnp.zeros((cs, hd), BF16)
    on_g = on_ref[...]
    h0 = jb * 2
    si = lax.broadcasted_iota(jnp.int32, (cs, 3 * cs), 0)
    sj = lax.broadcasted_iota(jnp.int32, (cs, 3 * cs), 1) % cs
    low3 = jnp.where(si >= sj, 1.0, 0.0).astype(BF16)
    ui = lax.broadcasted_iota(jnp.int32, (3 * cs, cs), 0) % cs
    uj = lax.broadcasted_iota(jnp.int32, (3 * cs, cs), 1)
    up3 = jnp.where(ui <= uj, 1.0, 0.0).astype(BF16)

    def col(x, idx):
        return jnp.sum(jnp.where(lane == idx, x, 0.0), axis=1, keepdims=True)

    def bd(x):
        zero = jnp.zeros_like(x)
        return jnp.concatenate([jnp.where(left, x, zero), jnp.where(left, zero, x)], axis=0)

    def bd_wide(x):
        zero = jnp.zeros((x.shape[0], hd), x.dtype)
        return jnp.concatenate([jnp.concatenate([x[:, :hd], zero], axis=1),
                                jnp.concatenate([zero, x[:, hd:]], axis=1)], axis=0)

    def recurrence(c):
        st = s_ref[...]
        r = _dot(cq_ref[rslot, c], bd_wide(st.astype(BF16)))
        s_ref[...] = st * gl_ref[rslot, c] + bm_ref[rslot, c] - r[:hd]
        o = r[hd:] + om_ref[rslot, c]
        r0 = c * cs
        for hh in range(2):
            oh = o[:, hh * hd:(hh + 1) * hd]
            ms = jnp.mean(oh * oh, axis=-1, keepdims=True)
            zc = z_ref[r0:r0 + cs, hh * hd:(hh + 1) * hd]
            out = oh * lax.rsqrt(ms + EPS) * on_g * _silu(zc)
            o_ref[r0:r0 + cs, hh * hd:(hh + 1) * hd] = out.astype(o_ref.dtype)

    pending = list(range(nc))

    def run_recurrence(k):
        for _ in range(k):
            if pending:
                recurrence(pending.pop(0))

    ba = ba_ref[...]
    beta_all = jax.nn.sigmoid(ba)
    g_all = -jnp.exp(alr_ref[...]) * _softplus(ba + dtr_ref[...])
    g_row = -jnp.exp(alc_ref[0]) * _softplus(at_ref[0, 0] + dtc_ref[0])
    gcs, grs = [], []
    for c in range(nc):
        r0 = c * cs
        gcs.append(_dot(low3, jnp.concatenate(_split3(g_all[r0:r0 + cs]), axis=0)))
        grs.append(_dot(jnp.concatenate(_split3(g_row[:, r0:r0 + cs]), axis=1), up3))
    run_recurrence(max(1, nc // 4))

    chunks = []
    for c in range(nc):
        r0 = c * cs
        b0 = col(beta_all[r0:r0 + cs], h0)
        b1 = col(beta_all[r0:r0 + cs], h0 + 1)
        g0 = col(gcs[c], n_heads + h0)
        g1 = col(gcs[c], n_heads + h0 + 1)
        grow = jnp.concatenate([grs[c][0:1], grs[c][1:2]], axis=1)
        glast = jnp.where(left[0:1], grs[c][0:1, cs - 1:cs], grs[c][1:2, cs - 1:cs])
        gcol = jnp.where(left, g0, g1)
        bcol = jnp.where(left, b0, b1)
        dec = jnp.where(incl, jnp.exp(jnp.where(incl, gcol - grow, 0.0)), 0.0)
        qc = q_ref[r0:r0 + cs, :]
        kc = k_ref[r0:r0 + cs, :]
        kcb = kc.astype(BF16)
        qkk = _dot_nt(jnp.concatenate([qc.astype(BF16), kcb], axis=0),
                      jnp.concatenate([kcb, kcb], axis=0))
        attn = qkk[:cs] * dec
        m = bcol * qkk[cs:] * jnp.where(strict, dec, 0.0)
        b2 = jnp.where(left2, b0, b1)
        eg2 = jnp.where(left2, jnp.exp(g0), jnp.exp(g1))
        vb = (v_ref[r0:r0 + cs, :] * b2).astype(BF16)
        kbe = (jnp.concatenate([kc, kc], axis=1) * (b2 * eg2)).astype(BF16)
        rmat = jnp.concatenate(
            [jnp.concatenate([vb[:, :hd], zb, kbe[:, :hd], zb], axis=1),
             jnp.concatenate([zb, vb[:, hd:], zb, kbe[:, hd:]], axis=1)], axis=0)
        qg = jnp.concatenate([qc, qc], axis=1) * eg2
        ktc = kt_ref[:, r0:r0 + cs]
        kdt = jnp.concatenate([ktc, ktc], axis=1) * jnp.exp(glast - grow)
        gl_ref[wslot, c] = jnp.where(left_s, jnp.exp(glast[:, 0:1]), jnp.exp(glast[:, cs:cs + 1]))
        chunks.append(dict(m=m, ak=jnp.concatenate([attn, kdt], axis=0).astype(BF16),
                           rmat=rmat, qg=qg))

    ts = [eye - ch["m"] for ch in chunks]
    ps = [_split_hl(ch["m"]) for ch in chunks]
    k = 2
    while k < cs:
        p2s = [_dot(jnp.concatenate([ph, plo, ph], axis=1),
                    jnp.concatenate([bd(ph), bd(ph), bd(plo)], axis=0)) for ph, plo in ps]
        ps = [_split_hl(p2) for p2 in p2s]
        if k in (4, 16):
            run_recurrence(max(1, nc // 4))
        new_ts = []
        for (p2h, p2l), t in zip(ps, ts):
            th, tl = _split_hl(t)
            rhs2 = jnp.concatenate([bd(p2h), bd(p2h), bd(p2l)], axis=0)
            new_ts.append(t + _dot(jnp.concatenate([th, tl, th], axis=1), rhs2))
        ts = new_ts
        k *= 2
    run_recurrence(max(1, nc // 4))

    uws = []
    for ch, t in zip(chunks, ts):
        th, tl = _split_hl(t)
        uws.append(_dot(jnp.concatenate([th, tl], axis=1),
                        jnp.concatenate([ch["rmat"], ch["rmat"]], axis=0)))
    run_recurrence(nc)

    for c, (ch, uw) in enumerate(zip(chunks, uws)):
        uwb = uw.astype(BF16)
        x = _dot(ch["ak"], jnp.concatenate([bd_wide(uwb[:, :2 * hd]), bd_wide(uwb[:, 2 * hd:])],
                                           axis=1))
        om_ref[wslot, c] = x[:cs, :2 * hd]
        bm_ref[wslot, c] = x[cs:, :2 * hd]
        cq_ref[wslot, c] = jnp.concatenate([x[cs:, 2 * hd:], ch["qg"] - x[:cs, 2 * hd:]],
                                           axis=0).astype(BF16)


def _gdn_core(qkv, k_t, proj, ba, a_t, a_log, dt_bias, onorm_g, batch, seq, qk_dim, v_dim, n_heads,
              head_dim, z_col0):
    t = qkv.shape[0]
    hd = head_dim
    qk_heads = qk_dim // hd
    hb = n_heads // qk_heads
    cs = GDN_CHUNK
    assert hb == 2 and 2 * cs == LANES and hd == LANES
    nblk = n_heads // hb
    tc = _pick(seq, (512, 256, 128, 64))
    nt = seq // tc
    nc = tc // cs
    pad = jnp.zeros((LANES - 2 * n_heads,), F32)
    zeros_h = jnp.zeros((n_heads,), F32)
    alr = jnp.concatenate([zeros_h, a_log.astype(F32), pad]).reshape(1, LANES)
    dtr = jnp.concatenate([zeros_h, dt_bias.astype(F32), pad]).reshape(1, LANES)
    alc = a_log.astype(F32).reshape(nblk, hb, 1)
    dtc = dt_bias.astype(F32).reshape(nblk, hb, 1)
    a_t = a_t.reshape(batch, nblk, hb, seq)
    kern = functools.partial(_gdn_kernel, n_heads=n_heads)
    k_off = qk_dim // hd
    v_off = 2 * qk_dim // (hb * hd)
    z_off = z_col0 // (hb * hd)

    def cur(b, n):
        return b * nt + jnp.minimum(n, nt - 1)

    def prev(b, n):
        return b * nt + jnp.maximum(n - 1, 0)

    return pl.pallas_call(
        kern,
        grid=(batch, nblk, nt + 1),
        in_specs=[pl.BlockSpec((tc, hd), lambda b, j, n: (cur(b, n), j)),
                  pl.BlockSpec((tc, hd), lambda b, j, n: (cur(b, n), k_off + j)),
                  pl.BlockSpec((hd, tc), lambda b, j, n: (j, cur(b, n))),
                  pl.BlockSpec((tc, hb * hd), lambda b, j, n: (cur(b, n), v_off + j)),
                  pl.BlockSpec((tc, hb * hd), lambda b, j, n: (prev(b, n), z_off + j)),
                  pl.BlockSpec((tc, LANES), lambda b, j, n: (cur(b, n), 0)),
                  pl.BlockSpec((1, 1, hb, tc), lambda b, j, n: (b, j, 0, jnp.minimum(n, nt - 1))),
                  pl.BlockSpec((1, LANES), lambda b, j, n: (0, 0)),
                  pl.BlockSpec((1, LANES), lambda b, j, n: (0, 0)),
                  pl.BlockSpec((1, hb, 1), lambda b, j, n: (j, 0, 0)),
                  pl.BlockSpec((1, hb, 1), lambda b, j, n: (j, 0, 0)),
                  pl.BlockSpec((1, hd), lambda b, j, n: (0, 0))],
        out_specs=pl.BlockSpec((tc, hb * hd), lambda b, j, n: (prev(b, n), j)),
        out_shape=jax.ShapeDtypeStruct((t, v_dim), BF16),
        scratch_shapes=[pltpu.VMEM((hd, hb * hd), F32),
                        pltpu.VMEM((2, nc, hd + cs, hb * hd), BF16),
                        pltpu.VMEM((2, nc, hd, hb * hd), F32),
                        pltpu.VMEM((2, nc, cs, hb * hd), F32),
                        pltpu.VMEM((2, nc, 1, hb * hd), F32)],
        compiler_params=_params(("parallel", "parallel", "arbitrary")),
        name="gdn_delta_rule",
    )(qkv, qkv, k_t, qkv, proj, ba, a_t, alr, dtr, alc, dtc, onorm_g.astype(F32).reshape(1, hd))


def _sb_kernel(q_ref, k_ref, v_ref, o_ref, acc_ref, lr_ref, zn_ref, wt_ref, *, tk):
    i = pl.program_id(2)
    tq = q_ref.shape[1]
    ks = 2 * tk
    nd = tq // ks

    acc_ref[...] = jnp.zeros_like(acc_ref)
    lr_ref[...] = jnp.zeros_like(lr_ref)

    ri = lax.broadcasted_iota(jnp.int32, (ks, ks), 0)
    ci = lax.broadcasted_iota(jnp.int32, (ks, ks), 1)
    usuf = jnp.where(ri >= ci, 1.0, 0.0).astype(BF16)

    def scores(q, start):
        return _dot_nt(q, k_ref[0, pl.ds(start, ks), :])

    def log_parts(zn, mask):
        nabs = lax.bitcast_convert_type(
            lax.bitcast_convert_type(zn, jnp.uint32) | jnp.uint32(0x80000000), F32)
        l1 = jnp.minimum(zn, 0.0) - jnp.log2(1.0 + jnp.exp2(nabs))
        if mask is not None:
            l1 = jnp.where(mask, l1, 0.0)
        return l1.astype(BF16)

    def suffix_sums(parts):
        return _dot(parts, usuf)

    def exp_weights(zns, sums, lr, mask):
        wts = []
        for zn, sm in zip(zns, sums):
            w = jnp.exp2(sm + jnp.concatenate([lr] * (ks // LANES), axis=1) - zn)
            if mask is not None:
                w = jnp.where(mask, w, 0.0)
            wts.append(w.astype(BF16))
            lr = lr + jnp.broadcast_to(sm[:, 0:1], lr.shape)
        return wts, lr

    base = pl.multiple_of(i * tq, tq)
    diag = []
    for dt in range(nd - 1, -1, -1):
        r0 = dt * ks
        start = pl.multiple_of(base + r0, ks)
        qi = lax.broadcasted_iota(jnp.int32, (tq - r0, ks), 0)
        ki = lax.broadcasted_iota(jnp.int32, (tq - r0, ks), 1)
        diag.append(dict(r0=r0, start=start, mask=ki < qi, zn=scores(q_ref[0, r0:tq, :], start)))
    for dg in diag:
        dg["parts"] = log_parts(dg["zn"], dg["mask"])
    for dg in diag:
        dg["sums"] = suffix_sums(dg["parts"])
    for dg in diag:
        r0 = dg["r0"]
        wts, lr = exp_weights([dg["zn"]], [dg["sums"]], lr_ref[r0:tq, :], dg["mask"])
        acc_ref[r0:tq, :] += _dot(wts[0], v_ref[0, pl.ds(dg["start"], ks), :])
        lr_ref[r0:tq, :] = lr

    def window(t, u):
        tile = jnp.maximum(i - 1 - t, 0)
        return pl.multiple_of(tile * tq + (nd - 1 - u) * ks, ks)

    q = q_ref[0]
    for u in range(nd):
        zn_ref[u] = scores(q, window(0, u))
    wt_ref[...] = jnp.zeros_like(wt_ref)

    def body(t, carry):
        tprev = jnp.maximum(t - 1, 0)
        nh = 2
        hr = tq // nh
        units = [(h, u) for u in range(nd) for h in range(nh)]
        lrs = [lr_ref[h * hr:(h + 1) * hr, :] for h in range(nh)]
        pvs = [None] * nh

        def zn_of(h, u):
            return zn_ref[u, h * hr:(h + 1) * hr, :]

        zcur = zn_of(*units[0])
        pcur = log_parts(zcur, None)
        for k, (h, u) in enumerate(units):
            rs = slice(h * hr, (h + 1) * hr)
            if k + 1 < len(units):
                znext = zn_of(*units[k + 1])
                pnext = log_parts(znext, None)
            term = _dot(wt_ref[u, rs, :], v_ref[0, pl.ds(window(tprev, u), ks), :])
            pvs[h] = term if pvs[h] is None else pvs[h] + term
            sums = suffix_sums(pcur)
            nxt = scores(q_ref[0, rs, :], window(t + 1, u))
            wts, lrs[h] = exp_weights([zcur], [sums], lrs[h], None)
            wt_ref[u, rs, :] = wts[0]
            zn_ref[u, rs, :] = nxt
            if k + 1 < len(units):
                zcur, pcur = znext, pnext
        for h in range(nh):
            acc_ref[h * hr:(h + 1) * hr, :] += pvs[h]
            lr_ref[h * hr:(h + 1) * hr, :] = lrs[h]
        return carry

    lax.fori_loop(0, i, body, 0)
    pv = None
    for u in range(nd):
        term = _dot(wt_ref[u], v_ref[0, pl.ds(window(jnp.maximum(i - 1, 0), u), ks), :])
        pv = term if pv is None else pv + term
    o_ref[0] = (acc_ref[...] + pv).astype(o_ref.dtype)


def _sb_attention(q, kv, head_dim):
    b, s, d = q.shape
    assert head_dim == LANES
    nh = d // head_dim
    tk = LANES
    tq = _pick(s, (1024, 512, 256))
    assert tq % (2 * tk) == 0
    return pl.pallas_call(
        functools.partial(_sb_kernel, tk=tk),
        grid=(b, nh, s // tq),
        in_specs=[pl.BlockSpec((1, tq, head_dim), lambda b_, h, i: (b_, i, h)),
                  pl.BlockSpec((1, s, head_dim), lambda b_, h, i: (b_, 0, h)),
                  pl.BlockSpec((1, s, head_dim), lambda b_, h, i: (b_, 0, nh + h))],
        out_specs=pl.BlockSpec((1, tq, head_dim), lambda b_, h, i: (b_, i, h)),
        out_shape=jax.ShapeDtypeStruct((b, s, d), BF16),
        scratch_shapes=[pltpu.VMEM((tq, head_dim), F32), pltpu.VMEM((tq, LANES), F32),
                        pltpu.VMEM((tq // (2 * tk), tq, 2 * tk), F32),
                        pltpu.VMEM((tq // (2 * tk), tq, 2 * tk), BF16)],
        compiler_params=_params(("parallel", "parallel", "arbitrary")),
        name="sb_attention",
    )(q, kv, kv)


def kernel(x, c, ada_w, ada_b, norm_g, ffn_w_gu, ffn_w_down, gdn_w_in, gdn_conv_w, gdn_a_log,
           gdn_dt_bias, gdn_onorm_g, gdn_w_out, kv_ada_w, kv_ada_b, kv_norm_g, kv_w, k_norm_g,
           sb_w_q, sb_q_norm_g, sb_w_out):
    batch, seq, d = x.shape
    depth = ada_w.shape[0]
    n_a = gdn_w_in.shape[0]
    n_heads = gdn_a_log.shape[1]
    hd = gdn_onorm_g.shape[1]
    v_dim = n_heads * hd
    conv_ch = gdn_conv_w.shape[2]
    qk_dim = (conv_ch - v_dim) // 2
    sb_hd = k_norm_g.shape[0]
    t = batch * seq
    assert batch <= SUBLANES and 2 * n_heads <= LANES

    c8 = jnp.pad(c.astype(F32), ((0, SUBLANES - batch), (0, 0)))
    mod_all = _ada(c8, ada_w, ada_b)[:, :batch].reshape(depth, batch, 3, 3, d)
    kv_mod = _ada(c8, kv_ada_w[None], kv_ada_b[None])[0, :batch].reshape(batch, 2, d)
    kv_mod = jnp.concatenate([kv_mod, jnp.zeros((batch, 1, d), F32)], axis=1)

    xf = x.reshape(t, d)
    kv = None
    for l in range(depth):
        xf = _ffn(xf, mod_all[l, :, 0], norm_g[l, 0], ffn_w_gu[l, 0].astype(BF16),
                  ffn_w_down[l, 0].astype(BF16), seq)
        mod = mod_all[l, :, 1]
        if l < n_a:
            w_in = gdn_w_in[l]
            n_main = conv_ch + v_dim
            w_ba = jnp.pad(w_in[:, n_main:], ((0, 0), (0, LANES - 2 * n_heads))).astype(BF16)
            proj, ba = _proj(xf, mod, norm_g[l, 1], w_in.astype(BF16), seq, F32, w_extra=w_ba,
                             n_cols=n_main)
            qkv = _gdn_conv(proj, gdn_conv_w[l], seq, qk_dim, hd)
            a_t = jnp.transpose(ba.reshape(batch, seq, LANES)[:, :, n_heads:2 * n_heads], (0, 2, 1))
            k_t = jnp.transpose(qkv[:, qk_dim:2 * qk_dim])
            og = _gdn_core(qkv, k_t, proj, ba, a_t, gdn_a_log[l], gdn_dt_bias[l], gdn_onorm_g[l],
                           batch, seq, qk_dim, v_dim, n_heads, hd, conv_ch)
            xf = _out_proj(og, gdn_w_out[l].astype(BF16), xf, mod, seq)
        else:
            jl = l - n_a
            q = _proj(xf, mod, norm_g[l, 1], sb_w_q[jl].astype(BF16), seq, BF16,
                      head_gain=sb_q_norm_g[jl], out_scale=-LOG2E * float(sb_hd) ** -0.5)
            o = _sb_attention(q.reshape(batch, seq, d), kv, sb_hd)
            xf = _out_proj(o.reshape(t, d), sb_w_out[jl].astype(BF16), xf, mod, seq)
        xf = _ffn(xf, mod_all[l, :, 2], norm_g[l, 2], ffn_w_gu[l, 1].astype(BF16),
                  ffn_w_down[l, 1].astype(BF16), seq)
        if l == n_a - 1:
            kv = _proj(xf, kv_mod, kv_norm_g, kv_w.astype(BF16), seq, BF16, head_gain=k_norm_g,
                       norm_cols=d).reshape(batch, seq, 2 * d)
    return xf.reshape(batch, seq, d)
```

```python
import functools

import jax
import jax.numpy as jnp
from jax import lax
from jax.experimental import pallas as pl
from jax.experimental.pallas import tpu as pltpu

F32 = jnp.float32
BF16 = jnp.bfloat16
EPS = 1e-6
LANES = 128
SUBLANES = 8
GDN_CHUNK = 64
VMEM_LIMIT = 56 * 1024 * 1024
LOG2E = 1.4426950408889634


def _pick(n, candidates):
    for c in candidates:
        if c <= n and n % c == 0:
            return c
    return n


def _params(sem):
    return pltpu.CompilerParams(dimension_semantics=sem, vmem_limit_bytes=VMEM_LIMIT)


def _silu(x):
    return x * jax.nn.sigmoid(x)


def _softplus(x):
    return jnp.maximum(x, 0.0) + jnp.log1p(jnp.exp(-jnp.abs(x)))


def _dot(a, b):
    return jnp.dot(a, b, preferred_element_type=F32)


def _dot_nt(a, b):
    return lax.dot_general(a, b, (((1,), (1,)), ((), ())), preferred_element_type=F32)


def _norm_mod(x, g, shift, scale):
    ms = jnp.mean(x * x, axis=-1, keepdims=True)
    y = x * lax.rsqrt(ms + EPS) * g
    return y * (1.0 + scale) + shift


def _ada_kernel(c_ref, w_ref, b_ref, o_ref):
    ca = _silu(c_ref[...])
    o_ref[0] = _dot(ca, w_ref[0]) + b_ref[0]


def _ada(c8, w, b):
    nl, d, n = w.shape
    tn = _pick(n, (1024, 512, 256, 128))
    return pl.pallas_call(
        _ada_kernel,
        grid=(nl, n // tn),
        in_specs=[pl.BlockSpec((SUBLANES, d), lambda l, j: (0, 0)),
                  pl.BlockSpec((1, d, tn), lambda l, j: (l, 0, j)),
                  pl.BlockSpec((1, 1, tn), lambda l, j: (l, 0, j))],
        out_specs=pl.BlockSpec((1, SUBLANES, tn), lambda l, j: (l, 0, j)),
        out_shape=jax.ShapeDtypeStruct((nl, SUBLANES, n), F32),
        compiler_params=_params(("parallel", "parallel")),
        name="ada_proj",
    )(c8, w, b.reshape(nl, 1, n))


def _ffn_kernel(x_ref, mod_ref, g_ref, wg_ref, wu_ref, wd_ref, o_ref, h_ref, acc_ref):
    j = pl.program_id(1)

    @pl.when(j == 0)
    def _():
        h = _norm_mod(x_ref[...], g_ref[...], mod_ref[0, 0:1, :], mod_ref[0, 1:2, :])
        h_ref[...] = h.astype(BF16)
        acc_ref[...] = jnp.zeros_like(acc_ref)

    h = h_ref[...]
    gate = _dot(h, wg_ref[...])
    up = _dot(h, wu_ref[...])
    a = (_silu(gate) * up).astype(BF16)
    acc_ref[...] += _dot(a, wd_ref[...])

    @pl.when(j == pl.num_programs(1) - 1)
    def _():
        o_ref[...] = x_ref[...] + 0.5 * (1.0 + mod_ref[0, 2:3, :]) * acc_ref[...]


def _ffn(x, mod, g, w_gu, w_down, widx, seq):
    t, d = x.shape
    f = w_down.shape[1]
    tm = _pick(seq, (512, 256, 128, 64, 32, 16, 8))
    tf = _pick(f, (512, 256, 128))
    tpb = seq // tm
    nf = f // tf
    return pl.pallas_call(
        _ffn_kernel,
        grid=(t // tm, nf),
        in_specs=[pl.BlockSpec((tm, d), lambda i, j: (i, 0)),
                  pl.BlockSpec((1, 3, d), lambda i, j: (i // tpb, 0, 0)),
                  pl.BlockSpec((1, d), lambda i, j: (0, 0)),
                  pl.BlockSpec((None, d, tf), lambda i, j: (widx, 0, j)),
                  pl.BlockSpec((None, d, tf), lambda i, j: (widx, 0, j + nf)),
                  pl.BlockSpec((None, tf, d), lambda i, j: (widx, j, 0))],
        out_specs=pl.BlockSpec((tm, d), lambda i, j: (i, 0)),
        out_shape=jax.ShapeDtypeStruct((t, d), F32),
        scratch_shapes=[pltpu.VMEM((tm, d), BF16), pltpu.VMEM((tm, d), F32)],
        compiler_params=_params(("parallel", "arbitrary")),
        name="ffn",
    )(x, mod, g.reshape(1, d), w_gu, w_gu, w_down)


def _proj_kernel(*refs, head_dim, out_scale, n_norm_tiles, has_extra):
    if has_extra:
        x_ref, mod_ref, g_ref, w_ref, hg_ref, wx_ref, o_ref, ox_ref, h_ref = refs
    else:
        x_ref, mod_ref, g_ref, w_ref, hg_ref, o_ref, h_ref = refs
    j = pl.program_id(1)

    @pl.when(j == 0)
    def _():
        h = _norm_mod(x_ref[...], g_ref[...], mod_ref[0, 0:1, :], mod_ref[0, 1:2, :])
        h_ref[...] = h.astype(BF16)
        if has_extra:
            ox_ref[...] = _dot(h_ref[...], wx_ref[...])

    y = _dot(h_ref[...], w_ref[...])

    def plain():
        o_ref[...] = y.astype(o_ref.dtype)

    def head_normed():
        for s in range(0, y.shape[1], head_dim):
            yh = y[:, s:s + head_dim]
            ms = jnp.mean(yh * yh, axis=-1, keepdims=True)
            yn = yh * lax.rsqrt(ms + EPS) * hg_ref[:, s:s + head_dim]
            o_ref[:, s:s + head_dim] = (yn * out_scale).astype(o_ref.dtype)

    if head_dim is None:
        plain()
    elif n_norm_tiles is None:
        head_normed()
    else:
        pl.when(j < n_norm_tiles)(head_normed)
        pl.when(j >= n_norm_tiles)(plain)


def _proj(x, mod, g, w, seq, out_dtype, head_gain=None, out_scale=1.0, norm_cols=None, w_extra=None,
          n_cols=None):
    t, d = x.shape
    n = w.shape[1] if n_cols is None else n_cols
    tm = _pick(seq, (512, 256, 128, 64, 32, 16, 8))
    tn = _pick(n if norm_cols is None else norm_cols, (1024, 512, 256, 128))
    assert n % tn == 0
    tpb = seq // tm
    if head_gain is None:
        head_dim = None
        hg = jnp.zeros((1, tn), F32)
    else:
        head_dim = head_gain.shape[0]
        hg = jnp.tile(head_gain.astype(F32), tn // head_dim).reshape(1, tn)
    n_norm_tiles = None if norm_cols is None else norm_cols // tn
    has_extra = w_extra is not None
    in_specs = [pl.BlockSpec((tm, d), lambda i, j: (i, 0)),
                pl.BlockSpec((1, 3, d), lambda i, j: (i // tpb, 0, 0)),
                pl.BlockSpec((1, d), lambda i, j: (0, 0)),
                pl.BlockSpec((d, tn), lambda i, j: (0, j)),
                pl.BlockSpec((1, tn), lambda i, j: (0, 0))]
    out_specs = pl.BlockSpec((tm, tn), lambda i, j: (i, j))
    out_shape = jax.ShapeDtypeStruct((t, n), out_dtype)
    args = [x, mod, g.reshape(1, d), w, hg]
    if has_extra:
        in_specs.append(pl.BlockSpec((d, LANES), lambda i, j: (0, 0)))
        out_specs = [out_specs, pl.BlockSpec((tm, LANES), lambda i, j: (i, 0))]
        out_shape = [out_shape, jax.ShapeDtypeStruct((t, LANES), F32)]
        args.append(w_extra)
    return pl.pallas_call(
        functools.partial(_proj_kernel, head_dim=head_dim, out_scale=out_scale,
                          n_norm_tiles=n_norm_tiles, has_extra=has_extra),
        grid=(t // tm, n // tn),
        in_specs=in_specs,
        out_specs=out_specs,
        out_shape=out_shape,
        scratch_shapes=[pltpu.VMEM((tm, d), BF16)],
        compiler_params=_params(("parallel", "arbitrary")),
        name="norm_proj",
    )(*args)


def _out_proj_kernel(a_ref, w_ref, x_ref, mod_ref, o_ref):
    y = _dot(a_ref[...], w_ref[...])
    o_ref[...] = x_ref[...] + (1.0 + mod_ref[0, 2:3, :]) * y


def _out_proj(a, w, x, mod, seq):
    t, k = a.shape
    d = w.shape[1]
    tm = _pick(seq, (512, 256, 128, 64, 32, 16, 8))
    tn = _pick(d, (1024, 512, 256, 128))
    tpb = seq // tm
    return pl.pallas_call(
        _out_proj_kernel,
        grid=(t // tm, d // tn),
        in_specs=[pl.BlockSpec((tm, k), lambda i, j: (i, 0)),
                  pl.BlockSpec((k, tn), lambda i, j: (0, j)),
                  pl.BlockSpec((tm, tn), lambda i, j: (i, j)),
                  pl.BlockSpec((1, 3, tn), lambda i, j: (i // tpb, 0, j))],
        out_specs=pl.BlockSpec((tm, tn), lambda i, j: (i, j)),
        out_shape=jax.ShapeDtypeStruct((t, d), F32),
        compiler_params=_params(("parallel", "arbitrary")),
        name="out_proj",
    )(a, w, x, mod)


def _conv_kernel(prev_ref, x_ref, w_ref, o_ref, kt_ref, xx_ref, *, tiles_per_batch, n_q_tiles,
                 n_qk_tiles, head_dim, q_scale):
    i = pl.program_id(0)
    j = pl.program_id(1)
    tm = x_ref.shape[0]
    kw = w_ref.shape[0]
    first = (i % tiles_per_batch) == 0
    xx_ref[0:SUBLANES, :] = jnp.where(first, 0.0, prev_ref[...])
    xx_ref[SUBLANES:SUBLANES + tm, :] = x_ref[...]
    y = None
    for tap in range(kw):
        off = SUBLANES - (kw - 1) + tap
        term = xx_ref[off:off + tm, :] * w_ref[tap:tap + 1, :]
        y = term if y is None else y + term
    y = _silu(y)

    @pl.when(j >= n_qk_tiles)
    def _():
        o_ref[...] = y

    @pl.when(j < n_qk_tiles)
    def _():
        scale = jnp.where(j < n_q_tiles, q_scale, 1.0)
        for s in range(0, y.shape[1], head_dim):
            yh = y[:, s:s + head_dim]
            ss = jnp.sum(yh * yh, axis=-1, keepdims=True)
            o_ref[:, s:s + head_dim] = yh * (lax.rsqrt(ss + EPS) * scale)

    @pl.when((j >= n_q_tiles) & (j < n_qk_tiles))
    def _():
        kt_ref[...] = o_ref[...].T


def _gdn_conv(proj, conv_w, seq, qk_dim, head_dim):
    t = proj.shape[0]
    kw, c = conv_w.shape
    tm = _pick(seq, (512, 256, 128, 64, 32, 16, 8))
    tc = _pick(qk_dim, (512, 256, 128))
    tpb = seq // tm
    rb = tm // SUBLANES
    nq = qk_dim // tc
    kern = functools.partial(_conv_kernel, tiles_per_batch=tpb, n_q_tiles=qk_dim // tc,
                             n_qk_tiles=2 * qk_dim // tc, head_dim=head_dim,
                             q_scale=float(head_dim) ** -0.5)
    return pl.pallas_call(
        kern,
        grid=(t // tm, c // tc),
        in_specs=[pl.BlockSpec((SUBLANES, tc), lambda i, j: (jnp.maximum(i * rb - 1, 0), j)),
                  pl.BlockSpec((tm, tc), lambda i, j: (i, j)),
                  pl.BlockSpec((kw, tc), lambda i, j: (0, j))],
        out_specs=[pl.BlockSpec((tm, tc), lambda i, j: (i, j)),
                   pl.BlockSpec((tc, tm), lambda i, j: (jnp.clip(j - nq, 0, nq - 1), i))],
        out_shape=[jax.ShapeDtypeStruct((t, c), F32), jax.ShapeDtypeStruct((qk_dim, t), F32)],
        scratch_shapes=[pltpu.VMEM((tm + SUBLANES, tc), F32)],
        compiler_params=_params(("arbitrary", "arbitrary")),
        name="gdn_conv",
    )(proj, proj, conv_w)


def _split_hl(a):
    hi = a.astype(BF16)
    lo = (a - hi.astype(F32)).astype(BF16)
    return hi, lo


def _split3(a):
    h1 = a.astype(BF16)
    r1 = a - h1.astype(F32)
    h2 = r1.astype(BF16)
    h3 = (r1 - h2.astype(F32)).astype(BF16)
    return h1, h2, h3


def _gdn_kernel(q_ref, k_ref, kt_ref, v_ref, z_ref, ba_ref, at_ref, alr_ref, dtr_ref, alc_ref,
                dtc_ref, on_ref, o_ref, s_ref, cq_ref, bm_ref, om_ref, gl_ref, *, n_heads):
    jb = pl.program_id(1)
    n = pl.program_id(2)
    tc = q_ref.shape[0]
    hd = q_ref.shape[1]
    cs = GDN_CHUNK
    nc = tc // cs
    wslot = n % 2
    rslot = 1 - wslot

    @pl.when(n == 0)
    def _():
        s_ref[...] = jnp.zeros_like(s_ref)
        cq_ref[1] = jnp.zeros_like(cq_ref[1])
        bm_ref[1] = jnp.zeros_like(bm_ref[1])
        om_ref[1] = jnp.zeros_like(om_ref[1])
        gl_ref[1] = jnp.zeros_like(gl_ref[1])

    lane = lax.broadcasted_iota(jnp.int32, (cs, 2 * cs), 1)
    left = lane < cs
    lane2 = lax.broadcasted_iota(jnp.int32, (cs, 2 * hd), 1)
    left2 = lane2 < hd
    left_s = lax.broadcasted_iota(jnp.int32, (1, 2 * hd), 1) < hd
    ii = lax.broadcasted_iota(jnp.int32, (cs, 2 * cs), 0)
    jj = lane & (cs - 1)
    incl = ii >= jj
    strict = ii > jj
    eye = jnp.where(ii == jj, 1.0, 0.0).astype(F32)
    zb = jnp.zeros((cs, hd), BF16)
    on_g = on_ref[...]
    h0 = jb * 2
    si = lax.broadcasted_iota(jnp.int32, (cs, 3 * cs), 0)
    sj = lax.broadcasted_iota(jnp.int32, (cs, 3 * cs), 1) % cs
    low3 = jnp.where(si >= sj, 1.0, 0.0).astype(BF16)
    ui = lax.broadcasted_iota(jnp.int32, (3 * cs, cs), 0) % cs
    uj = lax.broadcasted_iota(jnp.int32, (3 * cs, cs), 1)
    up3 = jnp.where(ui <= uj, 1.0, 0.0).astype(BF16)

    def col(x, idx):
        return jnp.sum(jnp.where(lane == idx, x, 0.0), axis=1, keepdims=True)

    def bd(x):
        zero = jnp.zeros_like(x)
        return jnp.concatenate([jnp.where(left, x, zero), jnp.where(left, zero, x)], axis=0)

    def bd_wide(x):
        zero = jnp.zeros((x.shape[0], hd), x.dtype)
        return jnp.concatenate([jnp.concatenate([x[:, :hd], zero], axis=1),
                                jnp.concatenate([zero, x[:, hd:]], axis=1)], axis=0)

    def recurrence(c):
        st = s_ref[...]
        r = _dot(cq_ref[rslot, c], bd_wide(st.astype(BF16)))
        s_ref[...] = st * gl_ref[rslot, c] + bm_ref[rslot, c] - r[:hd]
        o = r[hd:] + om_ref[rslot, c]
        r0 = c * cs
        for hh in range(2):
            oh = o[:, hh * hd:(hh + 1) * hd]
            ms = jnp.mean(oh * oh, axis=-1, keepdims=True)
            zc = z_ref[r0:r0 + cs, hh * hd:(hh + 1) * hd]
            out = oh * lax.rsqrt(ms + EPS) * on_g * _silu(zc)
            o_ref[r0:r0 + cs, hh * hd:(hh + 1) * hd] = out.astype(o_ref.dtype)

    pending = list(range(nc))

    def run_recurrence(k):
        for _ in range(k):
            if pending:
                recurrence(pending.pop(0))

    ba = ba_ref[...]
    beta_all = jax.nn.sigmoid(ba)
    g_all = -jnp.exp(alr_ref[...]) * _softplus(ba + dtr_ref[...])
    g_row = -jnp.exp(alc_ref[0]) * _softplus(at_ref[0, 0] + dtc_ref[0])
    gcs, grs = [], []
    for c in range(nc):
        r0 = c * cs
        gcs.append(_dot(low3, jnp.concatenate(_split3(g_all[r0:r0 + cs]), axis=0)))
        grs.append(_dot(jnp.concatenate(_split3(g_row[:, r0:r0 + cs]), axis=1), up3))
    run_recurrence(max(1, nc // 4))

    chunks = []
    for c in range(nc):
        r0 = c * cs
        b0 = col(beta_all[r0:r0 + cs], h0)
        b1 = col(beta_all[r0:r0 + cs], h0 + 1)
        g0 = col(gcs[c], n_heads + h0)
        g1 = col(gcs[c], n_heads + h0 + 1)
        grow = jnp.concatenate([grs[c][0:1], grs[c][1:2]], axis=1)
        glast = jnp.where(left[0:1], grs[c][0:1, cs - 1:cs], grs[c][1:2, cs - 1:cs])
        gcol = jnp.where(left, g0, g1)
        bcol = jnp.where(left, b0, b1)
        dec = jnp.where(incl, jnp.exp(jnp.where(incl, gcol - grow, 0.0)), 0.0)
        qc = q_ref[r0:r0 + cs, :]
        kc = k_ref[r0:r0 + cs, :]
        kcb = kc.astype(BF16)
        qkk = _dot_nt(jnp.concatenate([qc.astype(BF16), kcb], axis=0),
                      jnp.concatenate([kcb, kcb], axis=0))
        attn = qkk[:cs] * dec
        m = bcol * qkk[cs:] * jnp.where(strict, dec, 0.0)
        b2 = jnp.where(left2, b0, b1)
        eg2 = jnp.where(left2, jnp.exp(g0), jnp.exp(g1))
        vb = (v_ref[r0:r0 + cs, :] * b2).astype(BF16)
        kbe = (jnp.concatenate([kc, kc], axis=1) * (b2 * eg2)).astype(BF16)
        rmat = jnp.concatenate(
            [jnp.concatenate([vb[:, :hd], zb, kbe[:, :hd], zb], axis=1),
             jnp.concatenate([zb, vb[:, hd:], zb, kbe[:, hd:]], axis=1)], axis=0)
        qg = jnp.concatenate([qc, qc], axis=1) * eg2
        ktc = kt_ref[:, r0:r0 + cs]
        kdt = jnp.concatenate([ktc, ktc], axis=1) * jnp.exp(glast - grow)
        gl_ref[wslot, c] = jnp.where(left_s, jnp.exp(glast[:, 0:1]), jnp.exp(glast[:, cs:cs + 1]))
        chunks.append(dict(m=m, ak=jnp.concatenate([attn, kdt], axis=0).astype(BF16),
                           rmat=rmat, qg=qg))

    ts = [eye - ch["m"] for ch in chunks]
    ps = [_split_hl(ch["m"]) for ch in chunks]
    k = 2
    while k < cs:
        p2s = [_dot(jnp.concatenate([ph, plo, ph], axis=1),
                    jnp.concatenate([bd(ph), bd(ph), bd(plo)], axis=0)) for ph, plo in ps]
        ps = [_split_hl(p2) for p2 in p2s]
        if k in (4, 16):
            run_recurrence(max(1, nc // 4))
        new_ts = []
        for (p2h, p2l), t in zip(ps, ts):
            th, tl = _split_hl(t)
            rhs2 = jnp.concatenate([bd(p2h), bd(p2h), bd(p2l)], axis=0)
            new_ts.append(t + _dot(jnp.concatenate([th, tl, th], axis=1), rhs2))
        ts = new_ts
        k *= 2
    run_recurrence(max(1, nc // 4))

    uws = []
    for ch, t in zip(chunks, ts):
        th, tl = _split_hl(t)
        uws.append(_dot(jnp.concatenate([th, tl], axis=1),
                        jnp.concatenate([ch["rmat"], ch["rmat"]], axis=0)))
    run_recurrence(nc)

    for c, (ch, uw) in enumerate(zip(chunks, uws)):
        uwb = uw.astype(BF16)
        x = _dot(ch["ak"], jnp.concatenate([bd_wide(uwb[:, :2 * hd]), bd_wide(uwb[:, 2 * hd:])],
                                           axis=1))
        om_ref[wslot, c] = x[:cs, :2 * hd]
        bm_ref[wslot, c] = x[cs:, :2 * hd]
        cq_ref[wslot, c] = jnp.concatenate([x[cs:, 2 * hd:], ch["qg"] - x[:cs, 2 * hd:]],
                                           axis=0).astype(BF16)


def _gdn_core(qkv, k_t, proj, ba, a_t, a_log, dt_bias, onorm_g, batch, seq, qk_dim, v_dim, n_heads,
              head_dim, z_col0):
    t = qkv.shape[0]
    hd = head_dim
    qk_heads = qk_dim // hd
    hb = n_heads // qk_heads
    cs = GDN_CHUNK
    assert hb == 2 and 2 * cs == LANES and hd == LANES
    nblk = n_heads // hb
    tc = _pick(seq, (512, 256, 128, 64))
    nt = seq // tc
    nc = tc // cs
    pad = jnp.zeros((LANES - 2 * n_heads,), F32)
    zeros_h = jnp.zeros((n_heads,), F32)
    alr = jnp.concatenate([zeros_h, a_log.astype(F32), pad]).reshape(1, LANES)
    dtr = jnp.concatenate([zeros_h, dt_bias.astype(F32), pad]).reshape(1, LANES)
    alc = a_log.astype(F32).reshape(nblk, hb, 1)
    dtc = dt_bias.astype(F32).reshape(nblk, hb, 1)
    a_t = a_t.reshape(batch, nblk, hb, seq)
    kern = functools.partial(_gdn_kernel, n_heads=n_heads)
    k_off = qk_dim // hd
    v_off = 2 * qk_dim // (hb * hd)
    z_off = z_col0 // (hb * hd)

    def cur(b, n):
        return b * nt + jnp.minimum(n, nt - 1)

    def prev(b, n):
        return b * nt + jnp.maximum(n - 1, 0)

    return pl.pallas_call(
        kern,
        grid=(batch, nblk, nt + 1),
        in_specs=[pl.BlockSpec((tc, hd), lambda b, j, n: (cur(b, n), j)),
                  pl.BlockSpec((tc, hd), lambda b, j, n: (cur(b, n), k_off + j)),
                  pl.BlockSpec((hd, tc), lambda b, j, n: (j, cur(b, n))),
                  pl.BlockSpec((tc, hb * hd), lambda b, j, n: (cur(b, n), v_off + j)),
                  pl.BlockSpec((tc, hb * hd), lambda b, j, n: (prev(b, n), z_off + j)),
                  pl.BlockSpec((tc, LANES), lambda b, j, n: (cur(b, n), 0)),
                  pl.BlockSpec((1, 1, hb, tc), lambda b, j, n: (b, j, 0, jnp.minimum(n, nt - 1))),
                  pl.BlockSpec((1, LANES), lambda b, j, n: (0, 0)),
                  pl.BlockSpec((1, LANES), lambda b, j, n: (0, 0)),
                  pl.BlockSpec((1, hb, 1), lambda b, j, n: (j, 0, 0)),
                  pl.BlockSpec((1, hb, 1), lambda b, j, n: (j, 0, 0)),
                  pl.BlockSpec((1, hd), lambda b, j, n: (0, 0))],
        out_specs=pl.BlockSpec((tc, hb * hd), lambda b, j, n: (prev(b, n), j)),
        out_shape=jax.ShapeDtypeStruct((t, v_dim), BF16),
        scratch_shapes=[pltpu.VMEM((hd, hb * hd), F32),
                        pltpu.VMEM((2, nc, hd + cs, hb * hd), BF16),
                        pltpu.VMEM((2, nc, hd, hb * hd), F32),
                        pltpu.VMEM((2, nc, cs, hb * hd), F32),
                        pltpu.VMEM((2, nc, 1, hb * hd), F32)],
        compiler_params=_params(("parallel", "parallel", "arbitrary")),
        name="gdn_delta_rule",
    )(qkv, qkv, k_t, qkv, proj, ba, a_t, alr, dtr, alc, dtc, onorm_g.astype(F32).reshape(1, hd))


def _sb_kernel(q_ref, k_ref, v_ref, o_ref, acc_ref, lr_ref, zn_ref, wt_ref, *, tk):
    i = pl.program_id(2)
    tq = q_ref.shape[1]
    ks = 2 * tk
    nd = tq // ks

    acc_ref[...] = jnp.zeros_like(acc_ref)
    lr_ref[...] = jnp.zeros_like(lr_ref)

    ri = lax.broadcasted_iota(jnp.int32, (ks, ks), 0)
    ci = lax.broadcasted_iota(jnp.int32, (ks, ks), 1)
    usuf = jnp.where(ri >= ci, 1.0, 0.0).astype(BF16)

    def scores(q, start):
        return _dot_nt(q, k_ref[0, pl.ds(start, ks), :])

    def log_parts(zn, mask):
        nabs = lax.bitcast_convert_type(
            lax.bitcast_convert_type(zn, jnp.uint32) | jnp.uint32(0x80000000), F32)
        l1 = jnp.minimum(zn, 0.0) - jnp.log2(1.0 + jnp.exp2(nabs))
        if mask is not None:
            l1 = jnp.where(mask, l1, 0.0)
        return l1.astype(BF16)

    def suffix_sums(parts):
        return _dot(parts, usuf)

    def exp_weights(zns, sums, lr, mask):
        wts = []
        for zn, sm in zip(zns, sums):
            w = jnp.exp2(sm + jnp.concatenate([lr] * (ks // LANES), axis=1) - zn)
            if mask is not None:
                w = jnp.where(mask, w, 0.0)
            wts.append(w.astype(BF16))
            lr = lr + jnp.broadcast_to(sm[:, 0:1], lr.shape)
        return wts, lr

    base = pl.multiple_of(i * tq, tq)
    diag = []
    for dt in range(nd - 1, -1, -1):
        r0 = dt * ks
        start = pl.multiple_of(base + r0, ks)
        qi = lax.broadcasted_iota(jnp.int32, (tq - r0, ks), 0)
        ki = lax.broadcasted_iota(jnp.int32, (tq - r0, ks), 1)
        diag.append(dict(r0=r0, start=start, mask=ki < qi, zn=scores(q_ref[0, r0:tq, :], start)))
    for dg in diag:
        dg["parts"] = log_parts(dg["zn"], dg["mask"])
    for dg in diag:
        dg["sums"] = suffix_sums(dg["parts"])
    for dg in diag:
        r0 = dg["r0"]
        wts, lr = exp_weights([dg["zn"]], [dg["sums"]], lr_ref[r0:tq, :], dg["mask"])
        acc_ref[r0:tq, :] += _dot(wts[0], v_ref[0, pl.ds(dg["start"], ks), :])
        lr_ref[r0:tq, :] = lr

    def window(t, u):
        tile = jnp.maximum(i - 1 - t, 0)
        return pl.multiple_of(tile * tq + (nd - 1 - u) * ks, ks)

    q = q_ref[0]
    for u in range(nd):
        zn_ref[u] = scores(q, window(0, u))
    wt_ref[...] = jnp.zeros_like(wt_ref)

    def body(t, carry):
        tprev = jnp.maximum(t - 1, 0)
        nh = 2
        hr = tq // nh
        units = [(h, u) for u in range(nd) for h in range(nh)]
        lrs = [lr_ref[h * hr:(h + 1) * hr, :] for h in range(nh)]
        pvs = [None] * nh

        def zn_of(h, u):
            return zn_ref[u, h * hr:(h + 1) * hr, :]

        zcur = zn_of(*units[0])
        pcur = log_parts(zcur, None)
        for k, (h, u) in enumerate(units):
            rs = slice(h * hr, (h + 1) * hr)
            if k + 1 < len(units):
                znext = zn_of(*units[k + 1])
                pnext = log_parts(znext, None)
            term = _dot(wt_ref[u, rs, :], v_ref[0, pl.ds(window(tprev, u), ks), :])
            pvs[h] = term if pvs[h] is None else pvs[h] + term
            sums = suffix_sums(pcur)
            nxt = scores(q_ref[0, rs, :], window(t + 1, u))
            wts, lrs[h] = exp_weights([zcur], [sums], lrs[h], None)
            wt_ref[u, rs, :] = wts[0]
            zn_ref[u, rs, :] = nxt
            if k + 1 < len(units):
                zcur, pcur = znext, pnext
        for h in range(nh):
            acc_ref[h * hr:(h + 1) * hr, :] += pvs[h]
            lr_ref[h * hr:(h + 1) * hr, :] = lrs[h]
        return carry

    lax.fori_loop(0, i, body, 0)
    pv = None
    for u in range(nd):
        term = _dot(wt_ref[u], v_ref[0, pl.ds(window(jnp.maximum(i - 1, 0), u), ks), :])
        pv = term if pv is None else pv + term
    o_ref[0] = (acc_ref[...] + pv).astype(o_ref.dtype)


def _sb_attention(q, kv, head_dim):
    b, s, d = q.shape
    assert head_dim == LANES
    nh = d // head_dim
    tk = LANES
    tq = _pick(s, (1024, 512, 256))
    assert tq % (2 * tk) == 0
    return pl.pallas_call(
        functools.partial(_sb_kernel, tk=tk),
        grid=(b, nh, s // tq),
        in_specs=[pl.BlockSpec((1, tq, head_dim), lambda b_, h, i: (b_, i, h)),
                  pl.BlockSpec((1, s, head_dim), lambda b_, h, i: (b_, 0, h)),
                  pl.BlockSpec((1, s, head_dim), lambda b_, h, i: (b_, 0, nh + h))],
        out_specs=pl.BlockSpec((1, tq, head_dim), lambda b_, h, i: (b_, i, h)),
        out_shape=jax.ShapeDtypeStruct((b, s, d), BF16),
        scratch_shapes=[pltpu.VMEM((tq, head_dim), F32), pltpu.VMEM((tq, LANES), F32),
                        pltpu.VMEM((tq // (2 * tk), tq, 2 * tk), F32),
                        pltpu.VMEM((tq // (2 * tk), tq, 2 * tk), BF16)],
        compiler_params=_params(("parallel", "parallel", "arbitrary")),
        name="sb_attention",
    )(q, kv, kv)


def kernel(x, c, ada_w, ada_b, norm_g, ffn_w_gu, ffn_w_down, gdn_w_in, gdn_conv_w, gdn_a_log,
           gdn_dt_bias, gdn_onorm_g, gdn_w_out, kv_ada_w, kv_ada_b, kv_norm_g, kv_w, k_norm_g,
           sb_w_q, sb_q_norm_g, sb_w_out):
    batch, seq, d = x.shape
    depth = ada_w.shape[0]
    n_a = gdn_w_in.shape[0]
    n_heads = gdn_a_log.shape[1]
    hd = gdn_onorm_g.shape[1]
    v_dim = n_heads * hd
    conv_ch = gdn_conv_w.shape[2]
    qk_dim = (conv_ch - v_dim) // 2
    sb_hd = k_norm_g.shape[0]
    t = batch * seq
    assert batch <= SUBLANES and 2 * n_heads <= LANES

    c8 = jnp.pad(c.astype(F32), ((0, SUBLANES - batch), (0, 0)))
    mod_all = _ada(c8, ada_w, ada_b)[:, :batch].reshape(depth, batch, 3, 3, d)
    kv_mod = _ada(c8, kv_ada_w[None], kv_ada_b[None])[0, :batch].reshape(batch, 2, d)
    kv_mod = jnp.concatenate([kv_mod, jnp.zeros((batch, 1, d), F32)], axis=1)

    xf = x.reshape(t, d)
    kv = None
    w_gu = ffn_w_gu.astype(BF16).reshape((2 * depth,) + ffn_w_gu.shape[2:])
    w_down = ffn_w_down.astype(BF16).reshape((2 * depth,) + ffn_w_down.shape[2:])
    for l in range(depth):
        xf = _ffn(xf, mod_all[l, :, 0], norm_g[l, 0], w_gu, w_down, 2 * l, seq)
        mod = mod_all[l, :, 1]
        if l < n_a:
            w_in = gdn_w_in[l]
            n_main = conv_ch + v_dim
            w_ba = jnp.pad(w_in[:, n_main:], ((0, 0), (0, LANES - 2 * n_heads))).astype(BF16)
            proj, ba = _proj(xf, mod, norm_g[l, 1], w_in.astype(BF16), seq, F32, w_extra=w_ba,
                             n_cols=n_main)
            qkv, k_t = _gdn_conv(proj, gdn_conv_w[l], seq, qk_dim, hd)
            a_t = jnp.transpose(ba.reshape(batch, seq, LANES)[:, :, n_heads:2 * n_heads], (0, 2, 1))
            og = _gdn_core(qkv, k_t, proj, ba, a_t, gdn_a_log[l], gdn_dt_bias[l], gdn_onorm_g[l],
                           batch, seq, qk_dim, v_dim, n_heads, hd, conv_ch)
            xf = _out_proj(og, gdn_w_out[l].astype(BF16), xf, mod, seq)
        else:
            jl = l - n_a
            q = _proj(xf, mod, norm_g[l, 1], sb_w_q[jl].astype(BF16), seq, BF16,
                      head_gain=sb_q_norm_g[jl], out_scale=-LOG2E * float(sb_hd) ** -0.5)
            o = _sb_attention(q.reshape(batch, seq, d), kv, sb_hd)
            xf = _out_proj(o.reshape(t, d), sb_w_out[jl].astype(BF16), xf, mod, seq)
        xf = _ffn(xf, mod_all[l, :, 2], norm_g[l, 2], w_gu, w_down, 2 * l + 1, seq)
        if l == n_a - 1:
            kv = _proj(xf, kv_mod, kv_norm_g, kv_w.astype(BF16), seq, BF16, head_gain=k_norm_g,
                       norm_cols=d).reshape(batch, seq, 2 * d)
    return xf.reshape(batch, seq, d)
```
